```python
import jax, jax.numpy as jnp
from jax import lax
import numpy as np

D_MODEL = 1024
BATCH = 1
SEQ = 16384
DEPTH = 1

N_HEADS = 8
QK_NOPE_DIM = 64
QK_ROPE_DIM = 32
V_HEAD_DIM = 64
Q_LORA_RANK = 384
KV_LORA_RANK = 256
QK_HEAD_DIM = QK_NOPE_DIM + QK_ROPE_DIM
ATTN_WIDTH = N_HEADS * V_HEAD_DIM
ROPE_BASE = 10000.0
Q_BLOCK = 128

POOL_WINDOWS = (2, 4, 8, 16)
POOL_GROUPS = len(POOL_WINDOWS)
POOL_GROUP_DIM = 128
POOL_WIDTH = POOL_GROUPS * POOL_GROUP_DIM

D_FF = ((8 * D_MODEL + 3 * 256 - 1) // (3 * 256)) * 256

NORM_EPS = 1e-6

IN_SPLITS = (Q_LORA_RANK, KV_LORA_RANK, QK_ROPE_DIM, POOL_WIDTH, D_MODEL, D_MODEL)
IN_WIDTH = sum(IN_SPLITS)

kernel_name = "hybrid_mla_pool_gated_encoder_block"


def rms_norm(x, g):
    xf = x.astype(jnp.float32)
    y = xf * lax.rsqrt(jnp.mean(xf * xf, axis=-1, keepdims=True) + NORM_EPS)
    return (y * g.astype(jnp.float32)).astype(x.dtype)


def rope_tables(positions):
    inv = 1.0 / (ROPE_BASE ** (jnp.arange(0, QK_ROPE_DIM, 2, dtype=jnp.float32) / QK_ROPE_DIM))
    ang = positions.astype(jnp.float32)[..., None] * inv
    return jnp.cos(ang), jnp.sin(ang)


def apply_rope(t, cos, sin):
    tf = t.astype(jnp.float32)
    t1, t2 = jnp.split(tf, 2, axis=-1)
    out = jnp.concatenate([t1 * cos - t2 * sin, t2 * cos + t1 * sin], axis=-1)
    return out.astype(t.dtype)


def split_cols(p):
    offs = np.cumsum((0,) + IN_SPLITS)
    return [p[..., int(offs[i]):int(offs[i + 1])] for i in range(len(IN_SPLITS))]


def mla_attention(q_nope, q_rope, k_nope, k_rope, v):
    B, S, H, _ = q_nope.shape
    nb = S // Q_BLOCK
    scale = QK_HEAD_DIM ** -0.5

    def block(args):
        qn, qr = args
        s = (jnp.einsum('bqhd,bkhd->bhqk', qn, k_nope)
             + jnp.einsum('bqhr,bkr->bhqk', qr, k_rope))
        p = jax.nn.softmax(s.astype(jnp.float32) * scale, axis=-1).astype(v.dtype)
        return jnp.einsum('bhqk,bkhd->bqhd', p, v)

    qn_b = q_nope.reshape(B, nb, Q_BLOCK, H, QK_NOPE_DIM).transpose(1, 0, 2, 3, 4)
    qr_b = q_rope.reshape(B, nb, Q_BLOCK, H, QK_ROPE_DIM).transpose(1, 0, 2, 3, 4)
    out = lax.map(block, (qn_b, qr_b))
    return out.transpose(1, 0, 2, 3, 4).reshape(B, S, H * V_HEAD_DIM)


def multiscale_pool(u, w_pool_group, pool_scale):
    B, S, _ = u.shape
    uf = u.astype(jnp.float32)
    t = jnp.arange(S)
    outs = []
    for g, w in enumerate(POOL_WINDOWS):
        left = w // 2
        right = w - left - 1
        seg = uf[..., g * POOL_GROUP_DIM:(g + 1) * POOL_GROUP_DIM]
        cs = jnp.cumsum(jnp.pad(seg, ((0, 0), (left + 1, right), (0, 0))), axis=1)
        win_sum = cs[:, w:w + S] - cs[:, :S]
        cnt = (jnp.minimum(t + right, S - 1) - jnp.maximum(t - left, 0) + 1).astype(jnp.float32)
        outs.append(win_sum / cnt[None, :, None] - seg)
    pooled = jnp.stack(outs, axis=2).astype(u.dtype)
    mixed = jnp.einsum('bsgc,gcd->bsgd', pooled, w_pool_group).reshape(B, S, POOL_WIDTH)
    return mixed * pool_scale


def setup_inputs(seed: int = 0) -> dict:
    key = jax.random.key(seed)
    ks = jax.random.split(key, 20)
    f32 = jnp.float32

    def w(k, shape, fan_in):
        return jax.random.normal(k, shape, f32) * (fan_in ** -0.5)

    def gain(k, n):
        return 1.0 + 0.02 * jax.random.normal(k, (n,), f32)

    x = jax.random.normal(ks[0], (BATCH, SEQ, D_MODEL), f32)
    positions = jnp.broadcast_to(jnp.arange(SEQ, dtype=jnp.int32)[None, :], (BATCH, SEQ))
    return {
        "x": x,
        "positions": positions,
        "g_mix_pre": gain(ks[1], D_MODEL),
        "w_in": w(ks[2], (D_MODEL, IN_WIDTH), D_MODEL),
        "g_q_lat": gain(ks[3], Q_LORA_RANK),
        "w_uq": w(ks[4], (Q_LORA_RANK, N_HEADS * QK_HEAD_DIM), Q_LORA_RANK),
        "g_kv_lat": gain(ks[5], KV_LORA_RANK),
        "w_ukv": w(ks[6], (KV_LORA_RANK, N_HEADS * (QK_NOPE_DIM + V_HEAD_DIM)), KV_LORA_RANK),
        "w_o_attn": w(ks[7], (ATTN_WIDTH, D_MODEL), ATTN_WIDTH),
        "w_pool_group": w(ks[8], (POOL_GROUPS, POOL_GROUP_DIM, POOL_GROUP_DIM), POOL_GROUP_DIM),
        "pool_scale": gain(ks[9], POOL_WIDTH),
        "w_o_pool": w(ks[10], (POOL_WIDTH, D_MODEL), POOL_WIDTH),
        "w_out": w(ks[11], (D_MODEL, D_MODEL), D_MODEL),
        "g_mix_post": gain(ks[12], D_MODEL),
        "g_ffn_pre": gain(ks[13], D_MODEL),
        "w_gate_up": w(ks[14], (D_MODEL, 2 * D_FF), D_MODEL),
        "w_down": w(ks[15], (D_FF, D_MODEL), D_FF),
        "g_ffn_post": gain(ks[16], D_MODEL),
    }


def reference(x, positions, g_mix_pre, w_in, g_q_lat, w_uq, g_kv_lat, w_ukv, w_o_attn,
              w_pool_group, pool_scale, w_o_pool, w_out, g_mix_post, g_ffn_pre,
              w_gate_up, w_down, g_ffn_post):
    B, S, _ = x.shape
    cos, sin = rope_tables(positions)
    for _layer in range(DEPTH):
        h = rms_norm(x, g_mix_pre)
        c_q, c_kv, k_rope, u_pool, gate_a, gate_b = split_cols(h @ w_in)

        q = (rms_norm(c_q, g_q_lat) @ w_uq).reshape(B, S, N_HEADS, QK_HEAD_DIM)
        q_nope = q[..., :QK_NOPE_DIM]
        q_rope = apply_rope(q[..., QK_NOPE_DIM:], cos[:, :, None, :], sin[:, :, None, :])
        kv = (rms_norm(c_kv, g_kv_lat) @ w_ukv).reshape(B, S, N_HEADS, QK_NOPE_DIM + V_HEAD_DIM)
        k_nope = kv[..., :QK_NOPE_DIM]
        v = kv[..., QK_NOPE_DIM:]
        k_rope = apply_rope(k_rope, cos, sin)
        attn = mla_attention(q_nope, q_rope, k_nope, k_rope, v)

        pool = multiscale_pool(u_pool, w_pool_group, pool_scale)

        merged = (jax.nn.sigmoid(gate_a) * (attn @ w_o_attn)
                  + jax.nn.sigmoid(gate_b) * (pool @ w_o_pool))
        x = x + rms_norm(merged @ w_out, g_mix_post)

        hf = rms_norm(x, g_ffn_pre)
        gu = hf @ w_gate_up
        ff = (jax.nn.silu(gu[..., :D_FF]) * gu[..., D_FF:]) @ w_down
        x = x + rms_norm(ff, g_ffn_post)
    return x
```

```python
import functools
import math

import jax
import jax.numpy as jnp
from jax import lax
from jax.experimental import pallas as pl
from jax.experimental.pallas import tpu as pltpu

D_MODEL = 1024
N_HEADS = 8
QK_NOPE_DIM = 64
QK_ROPE_DIM = 32
V_HEAD_DIM = 64
Q_LORA_RANK = 384
KV_LORA_RANK = 256
QK_HEAD_DIM = QK_NOPE_DIM + QK_ROPE_DIM
ROPE_BASE = 10000.0
POOL_WINDOWS = (2, 4, 8, 16)
POOL_GROUP_DIM = 128
POOL_WIDTH = len(POOL_WINDOWS) * POOL_GROUP_DIM
D_FF = 2816
NORM_EPS = 1e-6

LANES = 128
SUBLANES = 8
HALO = 8
HEAD_W = N_HEADS * LANES
VMEM_LIMIT = 56 * 1024 * 1024

Q_SCALE = (QK_HEAD_DIM ** -0.5) * math.log2(math.e)
NEG_INIT = -1e30

PRE_TM = 512
ATTN_TQ = 512
ATTN_TK = 1024
MIX_TM = 512
FFN_TM = 256

F32 = jnp.float32
BF16 = jnp.bfloat16


def _rms(x, g):
    return x * lax.rsqrt(jnp.mean(x * x, axis=-1, keepdims=True) + NORM_EPS) * g


def _dot(a, b):
    return jnp.dot(a, b, preferred_element_type=F32)


def _pre_kernel(x_ref, pos_ref, g_pre_ref, w_in_ref, g_q_ref, w_uq_ref, g_kv_ref,
                w_uk_ref, w_uv_ref, inv_ref,
                q_ref, k_ref, v_ref, u_ref, ga_ref, gb_ref):
    h = _rms(x_ref[...], g_pre_ref[...]).astype(BF16)
    p = _dot(h, w_in_ref[...])
    o = 0
    c_q = p[:, o:o + Q_LORA_RANK]; o += Q_LORA_RANK
    c_kv = p[:, o:o + KV_LORA_RANK]; o += KV_LORA_RANK
    kr = p[:, o:o + LANES]; o += LANES
    u_ref[...] = p[:, o:o + POOL_WIDTH]; o += POOL_WIDTH
    ga_ref[...] = jax.nn.sigmoid(p[:, o:o + D_MODEL]); o += D_MODEL
    gb_ref[...] = jax.nn.sigmoid(p[:, o:o + D_MODEL])

    ang = pos_ref[...].astype(F32) * inv_ref[...]
    cos = jnp.cos(ang)
    sin = jnp.sin(ang)
    lane = lax.broadcasted_iota(jnp.int32, (1, LANES), 1)
    half = QK_ROPE_DIM // 2
    first = (lane >= QK_NOPE_DIM) & (lane < QK_NOPE_DIM + half)
    second = (lane >= QK_NOPE_DIM + half) & (lane < QK_HEAD_DIM)
    t_c = jnp.where(lane < QK_NOPE_DIM, 1.0, jnp.where(lane < QK_HEAD_DIM, cos, 0.0))
    t_m = jnp.where(first, -sin, 0.0)
    t_p = jnp.where(second, sin, 0.0)

    def rope(t):
        ahead = pltpu.roll(t, LANES - half, axis=1)
        behind = pltpu.roll(t, half, axis=1)
        return t * t_c + ahead * t_m + behind * t_p

    q = _dot(_rms(c_q, g_q_ref[...]).astype(BF16), w_uq_ref[...])
    ckv_n = _rms(c_kv, g_kv_ref[...]).astype(BF16)
    k = _dot(ckv_n, w_uk_ref[...])
    v = _dot(ckv_n, w_uv_ref[...])
    kr = rope(kr)
    for hd in range(N_HEADS):
        sl = slice(hd * LANES, (hd + 1) * LANES)
        q_ref[hd] = (rope(q[:, sl]) * Q_SCALE).astype(BF16)
        k_ref[hd] = (k[:, sl] + kr).astype(BF16)
    for pr in range(N_HEADS // 2):
        v_ref[pr] = v[:, pr * LANES:(pr + 1) * LANES].astype(BF16)


def _pre_call(x, pos, g_pre, w_in_p, g_q, w_uq_p, g_kv, w_uk_p, w_uv, inv_lane):
    s = x.shape[0]
    tm = PRE_TM
    row = lambda w: pl.BlockSpec((tm, w), lambda i: (i, 0))
    full = lambda a: pl.BlockSpec(a.shape, lambda i: (0,) * a.ndim)
    heads = lambda n: pl.BlockSpec((n, tm, LANES), lambda i: (0, i, 0))
    return pl.pallas_call(
        _pre_kernel,
        grid=(s // tm,),
        in_specs=[row(D_MODEL), row(1), full(g_pre), full(w_in_p), full(g_q), full(w_uq_p),
                  full(g_kv), full(w_uk_p), full(w_uv), full(inv_lane)],
        out_specs=[heads(N_HEADS), heads(N_HEADS), heads(N_HEADS // 2),
                   row(POOL_WIDTH), row(D_MODEL), row(D_MODEL)],
        out_shape=[jax.ShapeDtypeStruct((N_HEADS, s, LANES), BF16),
                   jax.ShapeDtypeStruct((N_HEADS, s, LANES), BF16),
                   jax.ShapeDtypeStruct((N_HEADS // 2, s, LANES), BF16),
                   jax.ShapeDtypeStruct((s, POOL_WIDTH), F32),
                   jax.ShapeDtypeStruct((s, D_MODEL), F32),
                   jax.ShapeDtypeStruct((s, D_MODEL), F32)],
        compiler_params=pltpu.CompilerParams(
            dimension_semantics=("parallel",), vmem_limit_bytes=VMEM_LIMIT),
        name="pre_proj",
    )(x, pos, g_pre, w_in_p, g_q, w_uq_p, g_kv, w_uk_p, w_uv, inv_lane)


def _attn_kernel(q_ref, k_ref, v_ref, o_ref, m_ref, l_ref, acc_ref):
    j = pl.program_id(1)

    @pl.when(j == 0)
    def _init():
        m_ref[...] = jnp.full(m_ref.shape, NEG_INIT, F32)
        l_ref[...] = jnp.zeros(l_ref.shape, F32)
        acc_ref[...] = jnp.zeros(acc_ref.shape, F32)

    def head(hd, carry):
        q = q_ref[hd]
        k = k_ref[hd]
        vp = v_ref[hd // 2]
        s = lax.dot_general(q, k, (((1,), (1,)), ((), ())), preferred_element_type=F32)
        m_prev = m_ref[hd]
        m_new = jnp.maximum(m_prev, jnp.max(s, axis=1, keepdims=True))
        alpha = jnp.exp2(m_prev - m_new)
        p = jnp.exp2(s - m_new[:, :1])
        l_ref[hd] = alpha * l_ref[hd] + jnp.sum(p, axis=1, keepdims=True)
        acc_ref[hd] = alpha * acc_ref[hd] + _dot(p.astype(BF16), vp)
        m_ref[hd] = m_new
        return carry

    lax.fori_loop(0, N_HEADS, head, 0)

    @pl.when(j == pl.num_programs(1) - 1)
    def _finish():
        for hd in range(N_HEADS):
            lo = (hd % 2) * V_HEAD_DIM
            o = acc_ref[hd][:, lo:lo + V_HEAD_DIM] / l_ref[hd][:, :V_HEAD_DIM]
            o_ref[:, hd * V_HEAD_DIM:(hd + 1) * V_HEAD_DIM] = o.astype(o_ref.dtype)


def _attn_call(q, k, v):
    s = q.shape[1]
    tq, tk = ATTN_TQ, ATTN_TK
    return pl.pallas_call(
        _attn_kernel,
        grid=(s // tq, s // tk),
        in_specs=[pl.BlockSpec((N_HEADS, tq, LANES), lambda i, j: (0, i, 0)),
                  pl.BlockSpec((N_HEADS, tk, LANES), lambda i, j: (0, j, 0)),
                  pl.BlockSpec((N_HEADS // 2, tk, LANES), lambda i, j: (0, j, 0))],
        out_specs=pl.BlockSpec((tq, N_HEADS * V_HEAD_DIM), lambda i, j: (i, 0)),
        out_shape=jax.ShapeDtypeStruct((s, N_HEADS * V_HEAD_DIM), BF16),
        scratch_shapes=[pltpu.VMEM((N_HEADS, tq, LANES), F32),
                        pltpu.VMEM((N_HEADS, tq, LANES), F32),
                        pltpu.VMEM((N_HEADS, tq, LANES), F32)],
        compiler_params=pltpu.CompilerParams(
            dimension_semantics=("parallel", "arbitrary"), vmem_limit_bytes=VMEM_LIMIT),
        name="mla_attention",
    )(q, k, v)


def _mix_kernel(x_ref, attn_ref, u_ref, up_ref, un_ref, ga_ref, gb_ref,
                w_oa_ref, w_pg_ref, pscale_ref, w_ob_ref, w_out_ref, g_post_ref,
                o_ref, ext_ref, *, seq_len):
    i = pl.program_id(0)
    tm = u_ref.shape[0]
    ext_ref[0:HALO, :] = jnp.where(i > 0, up_ref[...], 0.0)
    ext_ref[HALO:HALO + tm, :] = u_ref[...]
    ext_ref[HALO + tm:, :] = jnp.where(i < pl.num_programs(0) - 1, un_ref[...], 0.0)

    t = i * tm + lax.broadcasted_iota(jnp.int32, (tm, 1), 0)
    mixed = []
    for g, w in enumerate(POOL_WINDOWS):
        left = w // 2
        right = w - left - 1
        cols = slice(g * POOL_GROUP_DIM, (g + 1) * POOL_GROUP_DIM)
        win = ext_ref[HALO - left:HALO - left + tm, cols]
        for d in range(-left + 1, right + 1):
            win = win + ext_ref[HALO + d:HALO + d + tm, cols]
        cnt = (jnp.minimum(t + right, seq_len - 1) - jnp.maximum(t - left, 0) + 1).astype(F32)
        pooled = win / cnt - u_ref[:, cols]
        mixed.append(_dot(pooled.astype(BF16), w_pg_ref[g]))
    pool = jnp.concatenate(mixed, axis=1) * pscale_ref[...]

    a = _dot(attn_ref[...], w_oa_ref[...])
    b = _dot(pool.astype(BF16), w_ob_ref[...])
    merged = ga_ref[...] * a + gb_ref[...] * b
    y = _dot(merged.astype(BF16), w_out_ref[...])
    o_ref[...] = x_ref[...] + _rms(y, g_post_ref[...])


def _mix_call(x, attn, u, ga, gb, w_oa, w_pg, pscale, w_ob, w_out, g_post):
    s = x.shape[0]
    tm = MIX_TM
    nb = tm // HALO
    last = s // HALO - 1
    row = lambda w: pl.BlockSpec((tm, w), lambda i: (i, 0))
    full = lambda a: pl.BlockSpec(a.shape, lambda i: (0,) * a.ndim)
    prev = pl.BlockSpec((HALO, POOL_WIDTH), lambda i: (jnp.maximum(i * nb - 1, 0), 0))
    nxt = pl.BlockSpec((HALO, POOL_WIDTH), lambda i: (jnp.minimum((i + 1) * nb, last), 0))
    return pl.pallas_call(
        functools.partial(_mix_kernel, seq_len=s),
        grid=(s // tm,),
        in_specs=[row(D_MODEL), row(POOL_WIDTH), row(POOL_WIDTH), prev, nxt,
                  row(D_MODEL), row(D_MODEL),
                  full(w_oa), full(w_pg), full(pscale), full(w_ob), full(w_out), full(g_post)],
        out_specs=row(D_MODEL),
        out_shape=jax.ShapeDtypeStruct((s, D_MODEL), F32),
        scratch_shapes=[pltpu.VMEM((tm + 2 * HALO, POOL_WIDTH), F32)],
        compiler_params=pltpu.CompilerParams(
            dimension_semantics=("parallel",), vmem_limit_bytes=VMEM_LIMIT),
        name="mix_merge",
    )(x, attn, u, u, u, ga, gb, w_oa, w_pg, pscale, w_ob, w_out, g_post)


def _ffn_kernel(x_ref, g_pre_ref, w_gu_ref, w_dn_ref, g_post_ref, o_ref):
    x = x_ref[...]
    hf = _rms(x, g_pre_ref[...]).astype(BF16)
    gu = _dot(hf, w_gu_ref[...])
    act = jax.nn.silu(gu[:, :D_FF]) * gu[:, D_FF:]
    ff = _dot(act.astype(BF16), w_dn_ref[...])
    o_ref[...] = x + _rms(ff, g_post_ref[...])


def _ffn_call(x, g_pre, w_gu, w_dn, g_post):
    s = x.shape[0]
    tm = FFN_TM
    row = pl.BlockSpec((tm, D_MODEL), lambda i: (i, 0))
    full = lambda a: pl.BlockSpec(a.shape, lambda i: (0,) * a.ndim)
    return pl.pallas_call(
        _ffn_kernel,
        grid=(s // tm,),
        in_specs=[row, full(g_pre), full(w_gu), full(w_dn), full(g_post)],
        out_specs=row,
        out_shape=jax.ShapeDtypeStruct((s, D_MODEL), F32),
        compiler_params=pltpu.CompilerParams(
            dimension_semantics=("parallel",), vmem_limit_bytes=VMEM_LIMIT),
        name="swiglu_ffn",
    )(x, g_pre, w_gu, w_dn, g_post)


def _head_tiles(w, per_head, keep):
    r = w.shape[0]
    w = w.reshape(r, N_HEADS, per_head)[:, :, :keep]
    return jnp.pad(w, ((0, 0), (0, 0), (0, LANES - keep))).reshape(r, HEAD_W)


def kernel(x, positions, g_mix_pre, w_in, g_q_lat, w_uq, g_kv_lat, w_ukv, w_o_attn,
           w_pool_group, pool_scale, w_o_pool, w_out, g_mix_post, g_ffn_pre,
           w_gate_up, w_down, g_ffn_post):
    b, s, d = x.shape
    assert b == 1 and d == D_MODEL
    assert s % max(PRE_TM, ATTN_TQ, ATTN_TK, MIX_TM, FFN_TM) == 0
    vec = lambda g: g.reshape(1, -1).astype(F32)

    o_kr = Q_LORA_RANK + KV_LORA_RANK
    zeros = lambda n: jnp.zeros((D_MODEL, n), w_in.dtype)
    w_in_p = jnp.concatenate(
        [w_in[:, :o_kr], zeros(QK_NOPE_DIM), w_in[:, o_kr:o_kr + QK_ROPE_DIM],
         zeros(LANES - QK_HEAD_DIM), w_in[:, o_kr + QK_ROPE_DIM:]], axis=1).astype(BF16)
    w_uq_p = _head_tiles(w_uq, QK_HEAD_DIM, QK_HEAD_DIM).astype(BF16)
    w_uk_p = _head_tiles(w_ukv, QK_NOPE_DIM + V_HEAD_DIM, QK_NOPE_DIM).astype(BF16)
    w_uv = w_ukv.reshape(KV_LORA_RANK, N_HEADS, QK_NOPE_DIM + V_HEAD_DIM)[:, :, QK_NOPE_DIM:]
    w_uv = w_uv.reshape(KV_LORA_RANK, N_HEADS * V_HEAD_DIM).astype(BF16)

    half = QK_ROPE_DIM // 2
    inv = 1.0 / (ROPE_BASE ** (jnp.arange(0, QK_ROPE_DIM, 2, dtype=F32) / QK_ROPE_DIM))
    inv_lane = jnp.concatenate(
        [jnp.zeros((QK_NOPE_DIM,), F32), inv, inv, jnp.zeros((LANES - QK_HEAD_DIM,), F32)])
    assert inv.shape == (half,)

    x2 = x.reshape(s, d)
    pos = positions.reshape(s, 1)
    q, k, v, u, ga, gb = _pre_call(x2, pos, vec(g_mix_pre), w_in_p, vec(g_q_lat), w_uq_p,
                                   vec(g_kv_lat), w_uk_p, w_uv, inv_lane.reshape(1, LANES))
    attn = _attn_call(q, k, v)
    x1 = _mix_call(x2, attn, u, ga, gb, w_o_attn.astype(BF16), w_pool_group.astype(BF16),
                   vec(pool_scale), w_o_pool.astype(BF16), w_out.astype(BF16), vec(g_mix_post))
    out = _ffn_call(x1, vec(g_ffn_pre), w_gate_up.astype(BF16), w_down.astype(BF16),
                    vec(g_ffn_post))
    return out.reshape(b, s, d)
```

```python
import functools
import math

import jax
import jax.numpy as jnp
from jax import lax
from jax.experimental import pallas as pl
from jax.experimental.pallas import tpu as pltpu

D_MODEL = 1024
N_HEADS = 8
QK_NOPE_DIM = 64
QK_ROPE_DIM = 32
V_HEAD_DIM = 64
Q_LORA_RANK = 384
KV_LORA_RANK = 256
QK_HEAD_DIM = QK_NOPE_DIM + QK_ROPE_DIM
ROPE_BASE = 10000.0
POOL_WINDOWS = (2, 4, 8, 16)
POOL_GROUP_DIM = 128
POOL_WIDTH = len(POOL_WINDOWS) * POOL_GROUP_DIM
D_FF = 2816
NORM_EPS = 1e-6

LANES = 128
SUBLANES = 8
HALO = 8
HEAD_W = N_HEADS * LANES
VMEM_LIMIT = 56 * 1024 * 1024

Q_SCALE = (QK_HEAD_DIM ** -0.5) * math.log2(math.e)
NEG_INIT = -1e30

PRE_TM = 512
ATTN_TQ = 512
ATTN_TK = 1024
MIX_TM = 512
FFN_TM = 256

F32 = jnp.float32
BF16 = jnp.bfloat16


def _rms(x, g):
    return x * lax.rsqrt(jnp.mean(x * x, axis=-1, keepdims=True) + NORM_EPS) * g


def _dot(a, b):
    return jnp.dot(a, b, preferred_element_type=F32)


def _pre_kernel(x_ref, pos_ref, g_pre_ref, w_in_ref, g_q_ref, w_uq_ref, g_kv_ref,
                w_uk_ref, w_uv_ref, inv_ref,
                q_ref, k_ref, v_ref, u_ref, ga_ref, gb_ref):
    h = _rms(x_ref[...], g_pre_ref[...]).astype(BF16)
    p = _dot(h, w_in_ref[...])
    o = 0
    c_q = p[:, o:o + Q_LORA_RANK]; o += Q_LORA_RANK
    c_kv = p[:, o:o + KV_LORA_RANK]; o += KV_LORA_RANK
    kr = p[:, o:o + LANES]; o += LANES
    u_ref[...] = p[:, o:o + POOL_WIDTH]; o += POOL_WIDTH
    ga_ref[...] = jax.nn.sigmoid(p[:, o:o + D_MODEL]); o += D_MODEL
    gb_ref[...] = jax.nn.sigmoid(p[:, o:o + D_MODEL])

    ang = pos_ref[...].astype(F32) * inv_ref[...]
    cos = jnp.cos(ang)
    sin = jnp.sin(ang)
    lane = lax.broadcasted_iota(jnp.int32, (1, LANES), 1)
    half = QK_ROPE_DIM // 2
    first = (lane >= QK_NOPE_DIM) & (lane < QK_NOPE_DIM + half)
    second = (lane >= QK_NOPE_DIM + half) & (lane < QK_HEAD_DIM)
    t_c = jnp.where(lane < QK_NOPE_DIM, 1.0, jnp.where(lane < QK_HEAD_DIM, cos, 0.0))
    t_m = jnp.where(first, -sin, 0.0)
    t_p = jnp.where(second, sin, 0.0)

    def rope(t):
        ahead = pltpu.roll(t, LANES - half, axis=1)
        behind = pltpu.roll(t, half, axis=1)
        return t * t_c + ahead * t_m + behind * t_p

    q = _dot(_rms(c_q, g_q_ref[...]).astype(BF16), w_uq_ref[...])
    ckv_n = _rms(c_kv, g_kv_ref[...]).astype(BF16)
    k = _dot(ckv_n, w_uk_ref[...])
    v = _dot(ckv_n, w_uv_ref[...])
    kr = rope(kr)
    for hd in range(N_HEADS):
        sl = slice(hd * LANES, (hd + 1) * LANES)
        q_ref[hd] = (rope(q[:, sl]) * Q_SCALE).T.astype(BF16)
        k_ref[hd] = (k[:, sl] + kr).astype(BF16)
    v_ref[...] = v.T.astype(BF16)


def _pre_call(x, pos, g_pre, w_in_p, g_q, w_uq_p, g_kv, w_uk_p, w_uv, inv_lane):
    s = x.shape[0]
    tm = PRE_TM
    row = lambda w: pl.BlockSpec((tm, w), lambda i: (i, 0))
    full = lambda a: pl.BlockSpec(a.shape, lambda i: (0,) * a.ndim)
    heads = lambda n: pl.BlockSpec((n, tm, LANES), lambda i: (0, i, 0))
    return pl.pallas_call(
        _pre_kernel,
        grid=(s // tm,),
        in_specs=[row(D_MODEL), row(1), full(g_pre), full(w_in_p), full(g_q), full(w_uq_p),
                  full(g_kv), full(w_uk_p), full(w_uv), full(inv_lane)],
        out_specs=[pl.BlockSpec((N_HEADS, LANES, tm), lambda i: (0, 0, i)), heads(N_HEADS),
                   pl.BlockSpec((N_HEADS * V_HEAD_DIM, tm), lambda i: (0, i)),
                   row(POOL_WIDTH), row(D_MODEL), row(D_MODEL)],
        out_shape=[jax.ShapeDtypeStruct((N_HEADS, LANES, s), BF16),
                   jax.ShapeDtypeStruct((N_HEADS, s, LANES), BF16),
                   jax.ShapeDtypeStruct((N_HEADS * V_HEAD_DIM, s), BF16),
                   jax.ShapeDtypeStruct((s, POOL_WIDTH), F32),
                   jax.ShapeDtypeStruct((s, D_MODEL), F32),
                   jax.ShapeDtypeStruct((s, D_MODEL), F32)],
        compiler_params=pltpu.CompilerParams(
            dimension_semantics=("parallel",), vmem_limit_bytes=VMEM_LIMIT),
        name="pre_proj",
    )(x, pos, g_pre, w_in_p, g_q, w_uq_p, g_kv, w_uk_p, w_uv, inv_lane)


def _attn_kernel(qt_ref, k_ref, vt_ref, o_ref, m_ref, l_ref, acc_ref):
    j = pl.program_id(1)

    @pl.when(j == 0)
    def _init():
        m_ref[...] = jnp.full(m_ref.shape, NEG_INIT, F32)
        l_ref[...] = jnp.zeros(l_ref.shape, F32)
        acc_ref[...] = jnp.zeros(acc_ref.shape, F32)

    s_next = _dot(k_ref[0], qt_ref[0])
    for hd in range(N_HEADS):
        rows = slice(hd * V_HEAD_DIM, (hd + 1) * V_HEAD_DIM)
        s = s_next
        if hd + 1 < N_HEADS:
            s_next = _dot(k_ref[hd + 1], qt_ref[hd + 1])
        m_prev = m_ref[hd]
        m_new = jnp.maximum(m_prev, jnp.max(s, axis=0, keepdims=True))
        alpha = jnp.exp2(m_prev - m_new)
        p = jnp.exp2(s - m_new)
        l_ref[hd] = alpha * l_ref[hd] + jnp.sum(p, axis=0, keepdims=True)
        acc_ref[rows, :] = alpha * acc_ref[rows, :] + _dot(vt_ref[rows, :], p.astype(BF16))
        m_ref[hd] = m_new

    @pl.when(j == pl.num_programs(1) - 1)
    def _finish():
        for hd in range(N_HEADS):
            rows = slice(hd * V_HEAD_DIM, (hd + 1) * V_HEAD_DIM)
            acc_ref[rows, :] = acc_ref[rows, :] / l_ref[hd]
        o_ref[...] = acc_ref[...].T.astype(o_ref.dtype)


def _attn_call(qt, k, vt):
    s = k.shape[1]
    tq, tk = ATTN_TQ, ATTN_TK
    width = N_HEADS * V_HEAD_DIM
    return pl.pallas_call(
        _attn_kernel,
        grid=(s // tq, s // tk),
        in_specs=[pl.BlockSpec((N_HEADS, LANES, tq), lambda i, j: (0, 0, i)),
                  pl.BlockSpec((N_HEADS, tk, LANES), lambda i, j: (0, j, 0)),
                  pl.BlockSpec((width, tk), lambda i, j: (0, j))],
        out_specs=pl.BlockSpec((tq, width), lambda i, j: (i, 0)),
        out_shape=jax.ShapeDtypeStruct((s, width), BF16),
        scratch_shapes=[pltpu.VMEM((N_HEADS, 1, tq), F32),
                        pltpu.VMEM((N_HEADS, 1, tq), F32),
                        pltpu.VMEM((width, tq), F32)],
        compiler_params=pltpu.CompilerParams(
            dimension_semantics=("parallel", "arbitrary"), vmem_limit_bytes=VMEM_LIMIT),
        name="mla_attention",
    )(qt, k, vt)


def _mix_kernel(x_ref, attn_ref, u_ref, up_ref, un_ref, ga_ref, gb_ref,
                w_oa_ref, w_pg_ref, pscale_ref, w_ob_ref, w_out_ref, g_post_ref,
                o_ref, ext_ref, *, seq_len):
    i = pl.program_id(0)
    tm = u_ref.shape[0]
    ext_ref[0:HALO, :] = jnp.where(i > 0, up_ref[...], 0.0)
    ext_ref[HALO:HALO + tm, :] = u_ref[...]
    ext_ref[HALO + tm:, :] = jnp.where(i < pl.num_programs(0) - 1, un_ref[...], 0.0)

    t = i * tm + lax.broadcasted_iota(jnp.int32, (tm, 1), 0)
    mixed = []
    for g, w in enumerate(POOL_WINDOWS):
        left = w // 2
        right = w - left - 1
        cols = slice(g * POOL_GROUP_DIM, (g + 1) * POOL_GROUP_DIM)
        win = ext_ref[HALO - left:HALO - left + tm, cols]
        for d in range(-left + 1, right + 1):
            win = win + ext_ref[HALO + d:HALO + d + tm, cols]
        cnt = (jnp.minimum(t + right, seq_len - 1) - jnp.maximum(t - left, 0) + 1).astype(F32)
        pooled = win / cnt - u_ref[:, cols]
        mixed.append(_dot(pooled.astype(BF16), w_pg_ref[g]))
    pool = jnp.concatenate(mixed, axis=1) * pscale_ref[...]

    a = _dot(attn_ref[...], w_oa_ref[...])
    b = _dot(pool.astype(BF16), w_ob_ref[...])
    merged = ga_ref[...] * a + gb_ref[...] * b
    y = _dot(merged.astype(BF16), w_out_ref[...])
    o_ref[...] = x_ref[...] + _rms(y, g_post_ref[...])


def _mix_call(x, attn, u, ga, gb, w_oa, w_pg, pscale, w_ob, w_out, g_post):
    s = x.shape[0]
    tm = MIX_TM
    nb = tm // HALO
    last = s // HALO - 1
    row = lambda w: pl.BlockSpec((tm, w), lambda i: (i, 0))
    full = lambda a: pl.BlockSpec(a.shape, lambda i: (0,) * a.ndim)
    prev = pl.BlockSpec((HALO, POOL_WIDTH), lambda i: (jnp.maximum(i * nb - 1, 0), 0))
    nxt = pl.BlockSpec((HALO, POOL_WIDTH), lambda i: (jnp.minimum((i + 1) * nb, last), 0))
    return pl.pallas_call(
        functools.partial(_mix_kernel, seq_len=s),
        grid=(s // tm,),
        in_specs=[row(D_MODEL), row(POOL_WIDTH), row(POOL_WIDTH), prev, nxt,
                  row(D_MODEL), row(D_MODEL),
                  full(w_oa), full(w_pg), full(pscale), full(w_ob), full(w_out), full(g_post)],
        out_specs=row(D_MODEL),
        out_shape=jax.ShapeDtypeStruct((s, D_MODEL), F32),
        scratch_shapes=[pltpu.VMEM((tm + 2 * HALO, POOL_WIDTH), F32)],
        compiler_params=pltpu.CompilerParams(
            dimension_semantics=("parallel",), vmem_limit_bytes=VMEM_LIMIT),
        name="mix_merge",
    )(x, attn, u, u, u, ga, gb, w_oa, w_pg, pscale, w_ob, w_out, g_post)


def _ffn_kernel(x_ref, g_pre_ref, w_gu_ref, w_dn_ref, g_post_ref, o_ref):
    x = x_ref[...]
    hf = _rms(x, g_pre_ref[...]).astype(BF16)
    gu = _dot(hf, w_gu_ref[...])
    act = jax.nn.silu(gu[:, :D_FF]) * gu[:, D_FF:]
    ff = _dot(act.astype(BF16), w_dn_ref[...])
    o_ref[...] = x + _rms(ff, g_post_ref[...])


def _ffn_call(x, g_pre, w_gu, w_dn, g_post):
    s = x.shape[0]
    tm = FFN_TM
    row = pl.BlockSpec((tm, D_MODEL), lambda i: (i, 0))
    full = lambda a: pl.BlockSpec(a.shape, lambda i: (0,) * a.ndim)
    return pl.pallas_call(
        _ffn_kernel,
        grid=(s // tm,),
        in_specs=[row, full(g_pre), full(w_gu), full(w_dn), full(g_post)],
        out_specs=row,
        out_shape=jax.ShapeDtypeStruct((s, D_MODEL), F32),
        compiler_params=pltpu.CompilerParams(
            dimension_semantics=("parallel",), vmem_limit_bytes=VMEM_LIMIT),
        name="swiglu_ffn",
    )(x, g_pre, w_gu, w_dn, g_post)


def _head_tiles(w, per_head, keep):
    r = w.shape[0]
    w = w.reshape(r, N_HEADS, per_head)[:, :, :keep]
    return jnp.pad(w, ((0, 0), (0, 0), (0, LANES - keep))).reshape(r, HEAD_W)


def kernel(x, positions, g_mix_pre, w_in, g_q_lat, w_uq, g_kv_lat, w_ukv, w_o_attn,
           w_pool_group, pool_scale, w_o_pool, w_out, g_mix_post, g_ffn_pre,
           w_gate_up, w_down, g_ffn_post):
    b, s, d = x.shape
    assert b == 1 and d == D_MODEL
    assert s % max(PRE_TM, ATTN_TQ, ATTN_TK, MIX_TM, FFN_TM) == 0
    vec = lambda g: g.reshape(1, -1).astype(F32)

    o_kr = Q_LORA_RANK + KV_LORA_RANK
    zeros = lambda n: jnp.zeros((D_MODEL, n), w_in.dtype)
    w_in_p = jnp.concatenate(
        [w_in[:, :o_kr], zeros(QK_NOPE_DIM), w_in[:, o_kr:o_kr + QK_ROPE_DIM],
         zeros(LANES - QK_HEAD_DIM), w_in[:, o_kr + QK_ROPE_DIM:]], axis=1).astype(BF16)
    w_uq_p = _head_tiles(w_uq, QK_HEAD_DIM, QK_HEAD_DIM).astype(BF16)
    w_uk_p = _head_tiles(w_ukv, QK_NOPE_DIM + V_HEAD_DIM, QK_NOPE_DIM).astype(BF16)
    w_uv = w_ukv.reshape(KV_LORA_RANK, N_HEADS, QK_NOPE_DIM + V_HEAD_DIM)[:, :, QK_NOPE_DIM:]
    w_uv = w_uv.reshape(KV_LORA_RANK, N_HEADS * V_HEAD_DIM).astype(BF16)

    half = QK_ROPE_DIM // 2
    inv = 1.0 / (ROPE_BASE ** (jnp.arange(0, QK_ROPE_DIM, 2, dtype=F32) / QK_ROPE_DIM))
    inv_lane = jnp.concatenate(
        [jnp.zeros((QK_NOPE_DIM,), F32), inv, inv, jnp.zeros((LANES - QK_HEAD_DIM,), F32)])
    assert inv.shape == (half,)

    x2 = x.reshape(s, d)
    pos = positions.reshape(s, 1)
    q, k, v, u, ga, gb = _pre_call(x2, pos, vec(g_mix_pre), w_in_p, vec(g_q_lat), w_uq_p,
                                   vec(g_kv_lat), w_uk_p, w_uv, inv_lane.reshape(1, LANES))
    attn = _attn_call(q, k, v)
    x1 = _mix_call(x2, attn, u, ga, gb, w_o_attn.astype(BF16), w_pool_group.astype(BF16),
                   vec(pool_scale), w_o_pool.astype(BF16), w_out.astype(BF16), vec(g_mix_post))
    out = _ffn_call(x1, vec(g_ffn_pre), w_gate_up.astype(BF16), w_down.astype(BF16),
                    vec(g_ffn_post))
    return out.reshape(b, s, d)
```

```python
import functools
import math

import jax
import jax.numpy as jnp
from jax import lax
from jax.experimental import pallas as pl
from jax.experimental.pallas import tpu as pltpu

D_MODEL = 1024
N_HEADS = 8
QK_NOPE_DIM = 64
QK_ROPE_DIM = 32
V_HEAD_DIM = 64
Q_LORA_RANK = 384
KV_LORA_RANK = 256
QK_HEAD_DIM = QK_NOPE_DIM + QK_ROPE_DIM
ROPE_BASE = 10000.0
POOL_WINDOWS = (2, 4, 8, 16)
POOL_GROUP_DIM = 128
POOL_WIDTH = len(POOL_WINDOWS) * POOL_GROUP_DIM
D_FF = 2816
NORM_EPS = 1e-6

LANES = 128
SUBLANES = 8
HALO = 8
HEAD_W = N_HEADS * LANES
VMEM_LIMIT = 56 * 1024 * 1024

Q_SCALE = (QK_HEAD_DIM ** -0.5) * math.log2(math.e)
NEG_INIT = -1e30

PRE_TM = 512
ATTN_TQ = 512
ATTN_TK = 1024
QK_LEAD = 1
PV_LAG = 1
BF16_ROWS = 16
VT_ROWS = V_HEAD_DIM + BF16_ROWS
MIX_TM = 512
FFN_TM = 256

F32 = jnp.float32
BF16 = jnp.bfloat16


def _rms(x, g):
    return x * lax.rsqrt(jnp.mean(x * x, axis=-1, keepdims=True) + NORM_EPS) * g


def _dot(a, b):
    return jnp.dot(a, b, preferred_element_type=F32)


def _pre_kernel(x_ref, pos_ref, g_pre_ref, w_in_ref, g_q_ref, w_uq_ref, g_kv_ref,
                w_uk_ref, w_uv_ref, inv_ref,
                q_ref, k_ref, v_ref, u_ref, ga_ref, gb_ref):
    h = _rms(x_ref[...], g_pre_ref[...]).astype(BF16)
    p = _dot(h, w_in_ref[...])
    o = 0
    c_q = p[:, o:o + Q_LORA_RANK]; o += Q_LORA_RANK
    c_kv = p[:, o:o + KV_LORA_RANK]; o += KV_LORA_RANK
    kr = p[:, o:o + LANES]; o += LANES
    u_ref[...] = p[:, o:o + POOL_WIDTH]; o += POOL_WIDTH
    ga_ref[...] = jax.nn.sigmoid(p[:, o:o + D_MODEL]); o += D_MODEL
    gb_ref[...] = jax.nn.sigmoid(p[:, o:o + D_MODEL])

    ang = pos_ref[...].astype(F32) * inv_ref[...]
    cos = jnp.cos(ang)
    sin = jnp.sin(ang)
    lane = lax.broadcasted_iota(jnp.int32, (1, LANES), 1)
    half = QK_ROPE_DIM // 2
    first = (lane >= QK_NOPE_DIM) & (lane < QK_NOPE_DIM + half)
    second = (lane >= QK_NOPE_DIM + half) & (lane < QK_HEAD_DIM)
    t_c = jnp.where(lane < QK_NOPE_DIM, 1.0, jnp.where(lane < QK_HEAD_DIM, cos, 0.0))
    t_m = jnp.where(first, -sin, 0.0)
    t_p = jnp.where(second, sin, 0.0)

    def rope(t):
        ahead = pltpu.roll(t, LANES - half, axis=1)
        behind = pltpu.roll(t, half, axis=1)
        return t * t_c + ahead * t_m + behind * t_p

    q = _dot(_rms(c_q, g_q_ref[...]).astype(BF16), w_uq_ref[...])
    ckv_n = _rms(c_kv, g_kv_ref[...]).astype(BF16)
    k = _dot(ckv_n, w_uk_ref[...])
    v = _dot(ckv_n, w_uv_ref[...])
    kr = rope(kr)
    for hd in range(N_HEADS):
        sl = slice(hd * LANES, (hd + 1) * LANES)
        q_ref[hd] = (rope(q[:, sl]) * Q_SCALE).T.astype(BF16)
        k_ref[hd] = (k[:, sl] + kr).astype(BF16)
    vt = v.T
    pad_row = lax.broadcasted_iota(jnp.int32, (VT_ROWS - V_HEAD_DIM, vt.shape[1]), 0)
    ones_pad = jnp.where(pad_row == 0, 1.0, 0.0).astype(BF16)
    for hd in range(N_HEADS):
        lo = hd * VT_ROWS
        v_ref[lo:lo + V_HEAD_DIM, :] = vt[hd * V_HEAD_DIM:(hd + 1) * V_HEAD_DIM, :].astype(BF16)
        v_ref[lo + V_HEAD_DIM:lo + VT_ROWS, :] = ones_pad


def _pre_call(x, pos, g_pre, w_in_p, g_q, w_uq_p, g_kv, w_uk_p, w_uv, inv_lane):
    s = x.shape[0]
    tm = PRE_TM
    row = lambda w: pl.BlockSpec((tm, w), lambda i: (i, 0))
    full = lambda a: pl.BlockSpec(a.shape, lambda i: (0,) * a.ndim)
    heads = lambda n: pl.BlockSpec((n, tm, LANES), lambda i: (0, i, 0))
    return pl.pallas_call(
        _pre_kernel,
        grid=(s // tm,),
        in_specs=[row(D_MODEL), row(1), full(g_pre), full(w_in_p), full(g_q), full(w_uq_p),
                  full(g_kv), full(w_uk_p), full(w_uv), full(inv_lane)],
        out_specs=[pl.BlockSpec((N_HEADS, LANES, tm), lambda i: (0, 0, i)), heads(N_HEADS),
                   pl.BlockSpec((N_HEADS * VT_ROWS, tm), lambda i: (0, i)),
                   row(POOL_WIDTH), row(D_MODEL), row(D_MODEL)],
        out_shape=[jax.ShapeDtypeStruct((N_HEADS, LANES, s), BF16),
                   jax.ShapeDtypeStruct((N_HEADS, s, LANES), BF16),
                   jax.ShapeDtypeStruct((N_HEADS * VT_ROWS, s), BF16),
                   jax.ShapeDtypeStruct((s, POOL_WIDTH), F32),
                   jax.ShapeDtypeStruct((s, D_MODEL), F32),
                   jax.ShapeDtypeStruct((s, D_MODEL), F32)],
        compiler_params=pltpu.CompilerParams(
            dimension_semantics=("parallel",), vmem_limit_bytes=VMEM_LIMIT),
        name="pre_proj",
    )(x, pos, g_pre, w_in_p, g_q, w_uq_p, g_kv, w_uk_p, w_uv, inv_lane)


def _attn_kernel(qt_ref, k_ref, vt_ref, o_ref, m_ref, acc_ref):
    j = pl.program_id(1)

    @pl.when(j == 0)
    def _init():
        m_ref[...] = jnp.full(m_ref.shape, NEG_INIT, F32)
        acc_ref[...] = jnp.zeros(acc_ref.shape, F32)

    scores, probs, alphas = {}, {}, {}
    for t in range(N_HEADS + QK_LEAD + PV_LAG):
        if t < N_HEADS:
            scores[t] = _dot(k_ref[t], qt_ref[t])
        hd = t - QK_LEAD
        if 0 <= hd < N_HEADS:
            s = scores.pop(hd)
            m_prev = m_ref[hd]
            m_new = jnp.maximum(m_prev, jnp.max(s, axis=0, keepdims=True))
            alphas[hd] = jnp.exp2(m_prev - m_new)
            probs[hd] = jnp.exp2(s - m_new).astype(BF16)
            m_ref[hd] = m_new
        hd = t - QK_LEAD - PV_LAG
        if 0 <= hd < N_HEADS:
            rows = slice(hd * VT_ROWS, (hd + 1) * VT_ROWS)
            pv = _dot(vt_ref[rows, :], probs.pop(hd))
            acc_ref[rows, :] = alphas.pop(hd) * acc_ref[rows, :] + pv

    @pl.when(j == pl.num_programs(1) - 1)
    def _finish():
        outs = []
        for hd in range(N_HEADS):
            lo = hd * VT_ROWS
            denom = acc_ref[lo + V_HEAD_DIM:lo + V_HEAD_DIM + 1, :]
            outs.append(acc_ref[lo:lo + V_HEAD_DIM, :] / denom)
        o_ref[...] = jnp.concatenate(outs, axis=0).T.astype(o_ref.dtype)


def _attn_call(qt, k, vt):
    s = k.shape[1]
    tq, tk = ATTN_TQ, ATTN_TK
    width = N_HEADS * V_HEAD_DIM
    return pl.pallas_call(
        _attn_kernel,
        grid=(s // tq, s // tk),
        in_specs=[pl.BlockSpec((N_HEADS, LANES, tq), lambda i, j: (0, 0, i)),
                  pl.BlockSpec((N_HEADS, tk, LANES), lambda i, j: (0, j, 0)),
                  pl.BlockSpec((N_HEADS * VT_ROWS, tk), lambda i, j: (0, j))],
        out_specs=pl.BlockSpec((tq, width), lambda i, j: (i, 0)),
        out_shape=jax.ShapeDtypeStruct((s, width), BF16),
        scratch_shapes=[pltpu.VMEM((N_HEADS, 1, tq), F32),
                        pltpu.VMEM((N_HEADS * VT_ROWS, tq), F32)],
        compiler_params=pltpu.CompilerParams(
            dimension_semantics=("parallel", "arbitrary"), vmem_limit_bytes=VMEM_LIMIT),
        name="mla_attention",
    )(qt, k, vt)


def _mix_kernel(x_ref, attn_ref, u_ref, up_ref, un_ref, ga_ref, gb_ref,
                w_oa_ref, w_pg_ref, pscale_ref, w_ob_ref, w_out_ref, g_post_ref,
                o_ref, ext_ref, *, seq_len):
    i = pl.program_id(0)
    tm = u_ref.shape[0]
    ext_ref[0:HALO, :] = jnp.where(i > 0, up_ref[...], 0.0)
    ext_ref[HALO:HALO + tm, :] = u_ref[...]
    ext_ref[HALO + tm:, :] = jnp.where(i < pl.num_programs(0) - 1, un_ref[...], 0.0)

    t = i * tm + lax.broadcasted_iota(jnp.int32, (tm, 1), 0)
    mixed = []
    for g, w in enumerate(POOL_WINDOWS):
        left = w // 2
        right = w - left - 1
        cols = slice(g * POOL_GROUP_DIM, (g + 1) * POOL_GROUP_DIM)
        win = ext_ref[HALO - left:HALO - left + tm, cols]
        for d in range(-left + 1, right + 1):
            win = win + ext_ref[HALO + d:HALO + d + tm, cols]
        cnt = (jnp.minimum(t + right, seq_len - 1) - jnp.maximum(t - left, 0) + 1).astype(F32)
        pooled = win / cnt - u_ref[:, cols]
        mixed.append(_dot(pooled.astype(BF16), w_pg_ref[g]))
    pool = jnp.concatenate(mixed, axis=1) * pscale_ref[...]

    a = _dot(attn_ref[...], w_oa_ref[...])
    b = _dot(pool.astype(BF16), w_ob_ref[...])
    merged = ga_ref[...] * a + gb_ref[...] * b
    y = _dot(merged.astype(BF16), w_out_ref[...])
    o_ref[...] = x_ref[...] + _rms(y, g_post_ref[...])


def _mix_call(x, attn, u, ga, gb, w_oa, w_pg, pscale, w_ob, w_out, g_post):
    s = x.shape[0]
    tm = MIX_TM
    nb = tm // HALO
    last = s // HALO - 1
    row = lambda w: pl.BlockSpec((tm, w), lambda i: (i, 0))
    full = lambda a: pl.BlockSpec(a.shape, lambda i: (0,) * a.ndim)
    prev = pl.BlockSpec((HALO, POOL_WIDTH), lambda i: (jnp.maximum(i * nb - 1, 0), 0))
    nxt = pl.BlockSpec((HALO, POOL_WIDTH), lambda i: (jnp.minimum((i + 1) * nb, last), 0))
    return pl.pallas_call(
        functools.partial(_mix_kernel, seq_len=s),
        grid=(s // tm,),
        in_specs=[row(D_MODEL), row(POOL_WIDTH), row(POOL_WIDTH), prev, nxt,
                  row(D_MODEL), row(D_MODEL),
                  full(w_oa), full(w_pg), full(pscale), full(w_ob), full(w_out), full(g_post)],
        out_specs=row(D_MODEL),
        out_shape=jax.ShapeDtypeStruct((s, D_MODEL), F32),
        scratch_shapes=[pltpu.VMEM((tm + 2 * HALO, POOL_WIDTH), F32)],
        compiler_params=pltpu.CompilerParams(
            dimension_semantics=("parallel",), vmem_limit_bytes=VMEM_LIMIT),
        name="mix_merge",
    )(x, attn, u, u, u, ga, gb, w_oa, w_pg, pscale, w_ob, w_out, g_post)


def _ffn_kernel(x_ref, g_pre_ref, w_gu_ref, w_dn_ref, g_post_ref, o_ref):
    x = x_ref[...]
    hf = _rms(x, g_pre_ref[...]).astype(BF16)
    gu = _dot(hf, w_gu_ref[...])
    act = jax.nn.silu(gu[:, :D_FF]) * gu[:, D_FF:]
    ff = _dot(act.astype(BF16), w_dn_ref[...])
    o_ref[...] = x + _rms(ff, g_post_ref[...])


def _ffn_call(x, g_pre, w_gu, w_dn, g_post):
    s = x.shape[0]
    tm = FFN_TM
    row = pl.BlockSpec((tm, D_MODEL), lambda i: (i, 0))
    full = lambda a: pl.BlockSpec(a.shape, lambda i: (0,) * a.ndim)
    return pl.pallas_call(
        _ffn_kernel,
        grid=(s // tm,),
        in_specs=[row, full(g_pre), full(w_gu), full(w_dn), full(g_post)],
        out_specs=row,
        out_shape=jax.ShapeDtypeStruct((s, D_MODEL), F32),
        compiler_params=pltpu.CompilerParams(
            dimension_semantics=("parallel",), vmem_limit_bytes=VMEM_LIMIT),
        name="swiglu_ffn",
    )(x, g_pre, w_gu, w_dn, g_post)


def _head_tiles(w, per_head, keep):
    r = w.shape[0]
    w = w.reshape(r, N_HEADS, per_head)[:, :, :keep]
    return jnp.pad(w, ((0, 0), (0, 0), (0, LANES - keep))).reshape(r, HEAD_W)


def kernel(x, positions, g_mix_pre, w_in, g_q_lat, w_uq, g_kv_lat, w_ukv, w_o_attn,
           w_pool_group, pool_scale, w_o_pool, w_out, g_mix_post, g_ffn_pre,
           w_gate_up, w_down, g_ffn_post):
    b, s, d = x.shape
    assert b == 1 and d == D_MODEL
    assert s % max(PRE_TM, ATTN_TQ, ATTN_TK, MIX_TM, FFN_TM) == 0
    vec = lambda g: g.reshape(1, -1).astype(F32)

    o_kr = Q_LORA_RANK + KV_LORA_RANK
    zeros = lambda n: jnp.zeros((D_MODEL, n), w_in.dtype)
    w_in_p = jnp.concatenate(
        [w_in[:, :o_kr], zeros(QK_NOPE_DIM), w_in[:, o_kr:o_kr + QK_ROPE_DIM],
         zeros(LANES - QK_HEAD_DIM), w_in[:, o_kr + QK_ROPE_DIM:]], axis=1).astype(BF16)
    w_uq_p = _head_tiles(w_uq, QK_HEAD_DIM, QK_HEAD_DIM).astype(BF16)
    w_uk_p = _head_tiles(w_ukv, QK_NOPE_DIM + V_HEAD_DIM, QK_NOPE_DIM).astype(BF16)
    w_uv = w_ukv.reshape(KV_LORA_RANK, N_HEADS, QK_NOPE_DIM + V_HEAD_DIM)[:, :, QK_NOPE_DIM:]
    w_uv = w_uv.reshape(KV_LORA_RANK, N_HEADS * V_HEAD_DIM).astype(BF16)

    half = QK_ROPE_DIM // 2
    inv = 1.0 / (ROPE_BASE ** (jnp.arange(0, QK_ROPE_DIM, 2, dtype=F32) / QK_ROPE_DIM))
    inv_lane = jnp.concatenate(
        [jnp.zeros((QK_NOPE_DIM,), F32), inv, inv, jnp.zeros((LANES - QK_HEAD_DIM,), F32)])
    assert inv.shape == (half,)

    x2 = x.reshape(s, d)
    pos = positions.reshape(s, 1)
    q, k, v, u, ga, gb = _pre_call(x2, pos, vec(g_mix_pre), w_in_p, vec(g_q_lat), w_uq_p,
                                   vec(g_kv_lat), w_uk_p, w_uv, inv_lane.reshape(1, LANES))
    attn = _attn_call(q, k, v)
    x1 = _mix_call(x2, attn, u, ga, gb, w_o_attn.astype(BF16), w_pool_group.astype(BF16),
                   vec(pool_scale), w_o_pool.astype(BF16), w_out.astype(BF16), vec(g_mix_post))
    out = _ffn_call(x1, vec(g_ffn_pre), w_gate_up.astype(BF16), w_down.astype(BF16),
                    vec(g_ffn_post))
    return out.reshape(b, s, d)
```

```python
import functools
import math

import jax
import jax.numpy as jnp
from jax import lax
from jax.experimental import pallas as pl
from jax.experimental.pallas import tpu as pltpu

D_MODEL = 1024
N_HEADS = 8
QK_NOPE_DIM = 64
QK_ROPE_DIM = 32
V_HEAD_DIM = 64
Q_LORA_RANK = 384
KV_LORA_RANK = 256
QK_HEAD_DIM = QK_NOPE_DIM + QK_ROPE_DIM
ROPE_BASE = 10000.0
POOL_WINDOWS = (2, 4, 8, 16)
POOL_GROUP_DIM = 128
POOL_WIDTH = len(POOL_WINDOWS) * POOL_GROUP_DIM
D_FF = 2816
NORM_EPS = 1e-6

LANES = 128
SUBLANES = 8
HALO = 8
HEAD_W = N_HEADS * LANES
VMEM_LIMIT = 56 * 1024 * 1024

Q_SCALE = (QK_HEAD_DIM ** -0.5) * math.log2(math.e)
NEG_INIT = -1e30

PRE_TM = 512
ATTN_TQ = 1024
ATTN_TK = 1024
QK_LEAD = 1
PV_LAG = 1
EXP_HEADROOM = 60.0
BF16_ROWS = 16
VT_ROWS = V_HEAD_DIM + BF16_ROWS
MIX_TM = 512
FFN_TM = 256

F32 = jnp.float32
BF16 = jnp.bfloat16


def _rms(x, g):
    return x * lax.rsqrt(jnp.mean(x * x, axis=-1, keepdims=True) + NORM_EPS) * g


def _dot(a, b):
    return jnp.dot(a, b, preferred_element_type=F32)


def _pre_kernel(x_ref, pos_ref, g_pre_ref, w_in_ref, g_q_ref, w_uq_ref, g_kv_ref,
                w_uk_ref, w_uv_ref, inv_ref,
                q_ref, k_ref, v_ref, u_ref, ga_ref, gb_ref, qn_ref, kn_ref):
    h = _rms(x_ref[...], g_pre_ref[...]).astype(BF16)
    p = _dot(h, w_in_ref[...])
    o = 0
    c_q = p[:, o:o + Q_LORA_RANK]; o += Q_LORA_RANK
    c_kv = p[:, o:o + KV_LORA_RANK]; o += KV_LORA_RANK
    kr = p[:, o:o + LANES]; o += LANES
    u_ref[...] = p[:, o:o + POOL_WIDTH]; o += POOL_WIDTH
    ga_ref[...] = jax.nn.sigmoid(p[:, o:o + D_MODEL]); o += D_MODEL
    gb_ref[...] = jax.nn.sigmoid(p[:, o:o + D_MODEL])

    ang = pos_ref[...].astype(F32) * inv_ref[...]
    cos = jnp.cos(ang)
    sin = jnp.sin(ang)
    lane = lax.broadcasted_iota(jnp.int32, (1, LANES), 1)
    half = QK_ROPE_DIM // 2
    first = (lane >= QK_NOPE_DIM) & (lane < QK_NOPE_DIM + half)
    second = (lane >= QK_NOPE_DIM + half) & (lane < QK_HEAD_DIM)
    t_c = jnp.where(lane < QK_NOPE_DIM, 1.0, jnp.where(lane < QK_HEAD_DIM, cos, 0.0))
    t_m = jnp.where(first, -sin, 0.0)
    t_p = jnp.where(second, sin, 0.0)

    def rope(t):
        ahead = pltpu.roll(t, LANES - half, axis=1)
        behind = pltpu.roll(t, half, axis=1)
        return t * t_c + ahead * t_m + behind * t_p

    q = _dot(_rms(c_q, g_q_ref[...]).astype(BF16), w_uq_ref[...])
    ckv_n = _rms(c_kv, g_kv_ref[...]).astype(BF16)
    k = _dot(ckv_n, w_uk_ref[...])
    v = _dot(ckv_n, w_uv_ref[...])
    kr = rope(kr)
    for hd in range(N_HEADS):
        sl = slice(hd * LANES, (hd + 1) * LANES)
        qt = (rope(q[:, sl]) * Q_SCALE).T.astype(BF16)
        kb = (k[:, sl] + kr).astype(BF16)
        q_ref[hd] = qt
        k_ref[hd] = kb
        qf = qt.astype(F32)
        kf = kb.astype(F32)
        qn_ref[hd:hd + 1, :] = jnp.sum(qf * qf, axis=0, keepdims=True)
        k_sq = jnp.max(jnp.sum(kf * kf, axis=1, keepdims=True), axis=0, keepdims=True)
        kn_ref[0, hd:hd + 1, :] = jnp.broadcast_to(k_sq, (1, LANES))
    vt = v.T
    pad_row = lax.broadcasted_iota(jnp.int32, (VT_ROWS - V_HEAD_DIM, vt.shape[1]), 0)
    ones_pad = jnp.where(pad_row == 0, 1.0, 0.0).astype(BF16)
    for hd in range(N_HEADS):
        lo = hd * VT_ROWS
        v_ref[lo:lo + V_HEAD_DIM, :] = vt[hd * V_HEAD_DIM:(hd + 1) * V_HEAD_DIM, :].astype(BF16)
        v_ref[lo + V_HEAD_DIM:lo + VT_ROWS, :] = ones_pad


def _pre_call(x, pos, g_pre, w_in_p, g_q, w_uq_p, g_kv, w_uk_p, w_uv, inv_lane):
    s = x.shape[0]
    tm = PRE_TM
    row = lambda w: pl.BlockSpec((tm, w), lambda i: (i, 0))
    full = lambda a: pl.BlockSpec(a.shape, lambda i: (0,) * a.ndim)
    heads = lambda n: pl.BlockSpec((n, tm, LANES), lambda i: (0, i, 0))
    return pl.pallas_call(
        _pre_kernel,
        grid=(s // tm,),
        in_specs=[row(D_MODEL), row(1), full(g_pre), full(w_in_p), full(g_q), full(w_uq_p),
                  full(g_kv), full(w_uk_p), full(w_uv), full(inv_lane)],
        out_specs=[pl.BlockSpec((N_HEADS, LANES, tm), lambda i: (0, 0, i)), heads(N_HEADS),
                   pl.BlockSpec((N_HEADS * VT_ROWS, tm), lambda i: (0, i)),
                   row(POOL_WIDTH), row(D_MODEL), row(D_MODEL),
                   pl.BlockSpec((N_HEADS, tm), lambda i: (0, i)),
                   pl.BlockSpec((1, N_HEADS, LANES), lambda i: (i, 0, 0))],
        out_shape=[jax.ShapeDtypeStruct((N_HEADS, LANES, s), BF16),
                   jax.ShapeDtypeStruct((N_HEADS, s, LANES), BF16),
                   jax.ShapeDtypeStruct((N_HEADS * VT_ROWS, s), BF16),
                   jax.ShapeDtypeStruct((s, POOL_WIDTH), F32),
                   jax.ShapeDtypeStruct((s, D_MODEL), F32),
                   jax.ShapeDtypeStruct((s, D_MODEL), F32),
                   jax.ShapeDtypeStruct((N_HEADS, s), F32),
                   jax.ShapeDtypeStruct((s // tm, N_HEADS, LANES), F32)],
        compiler_params=pltpu.CompilerParams(
            dimension_semantics=("parallel",), vmem_limit_bytes=VMEM_LIMIT),
        name="pre_proj",
    )(x, pos, g_pre, w_in_p, g_q, w_uq_p, g_kv, w_uk_p, w_uv, inv_lane)


def _attn_kernel(qt_ref, k_ref, vt_ref, qn_ref, kn_ref, o_ref, m_ref, acc_ref):
    j = pl.program_id(1)

    @pl.when(j == 0)
    def _init():
        m_ref[...] = jnp.full(m_ref.shape, NEG_INIT, F32)
        acc_ref[...] = jnp.zeros(acc_ref.shape, F32)

    def head_rows(hd):
        return slice(hd * VT_ROWS, (hd + 1) * VT_ROWS)

    k_norm = jnp.sqrt(jnp.max(kn_ref[...], axis=0)[:, :1])
    bound = jnp.sqrt(qn_ref[...]) * k_norm
    excess = jnp.max(bound - m_ref[...].reshape(bound.shape))
    streaming = excess <= EXP_HEADROOM

    @pl.when(streaming)
    def _stream():
        scores = {}
        for t in range(N_HEADS + QK_LEAD):
            if t < N_HEADS:
                scores[t] = _dot(k_ref[t], qt_ref[t])
            hd = t - QK_LEAD
            if hd >= 0:
                p = jnp.exp2(scores.pop(hd) - m_ref[hd]).astype(BF16)
                rows = head_rows(hd)
                acc_ref[rows, :] = acc_ref[rows, :] + _dot(vt_ref[rows, :], p)

    @pl.when(jnp.logical_not(streaming))
    def _two_pass():
        scores, probs, alphas = {}, {}, {}
        for t in range(N_HEADS + QK_LEAD + PV_LAG):
            if t < N_HEADS:
                scores[t] = _dot(k_ref[t], qt_ref[t])
            hd = t - QK_LEAD
            if 0 <= hd < N_HEADS:
                s = scores.pop(hd)
                m_prev = m_ref[hd]
                m_new = jnp.maximum(m_prev, jnp.max(s, axis=0, keepdims=True))
                alphas[hd] = jnp.exp2(m_prev - m_new)
                probs[hd] = jnp.exp2(s - m_new).astype(BF16)
                m_ref[hd] = m_new
            hd = t - QK_LEAD - PV_LAG
            if 0 <= hd < N_HEADS:
                rows = head_rows(hd)
                pv = _dot(vt_ref[rows, :], probs.pop(hd))
                acc_ref[rows, :] = alphas.pop(hd) * acc_ref[rows, :] + pv

    @pl.when(j == pl.num_programs(1) - 1)
    def _finish():
        outs = []
        for hd in range(N_HEADS):
            lo = hd * VT_ROWS
            denom = acc_ref[lo + V_HEAD_DIM:lo + V_HEAD_DIM + 1, :]
            outs.append(acc_ref[lo:lo + V_HEAD_DIM, :] / denom)
        o_ref[...] = jnp.concatenate(outs, axis=0).T.astype(o_ref.dtype)


def _attn_call(qt, k, vt, qn, kn):
    s = k.shape[1]
    tq, tk = ATTN_TQ, ATTN_TK
    width = N_HEADS * V_HEAD_DIM
    return pl.pallas_call(
        _attn_kernel,
        grid=(s // tq, s // tk),
        in_specs=[pl.BlockSpec((N_HEADS, LANES, tq), lambda i, j: (0, 0, i)),
                  pl.BlockSpec((N_HEADS, tk, LANES), lambda i, j: (0, j, 0)),
                  pl.BlockSpec((N_HEADS * VT_ROWS, tk), lambda i, j: (0, j)),
                  pl.BlockSpec((N_HEADS, tq), lambda i, j: (0, i)),
                  pl.BlockSpec((tk // PRE_TM, N_HEADS, LANES), lambda i, j: (j, 0, 0))],
        out_specs=pl.BlockSpec((tq, width), lambda i, j: (i, 0)),
        out_shape=jax.ShapeDtypeStruct((s, width), BF16),
        scratch_shapes=[pltpu.VMEM((N_HEADS, 1, tq), F32),
                        pltpu.VMEM((N_HEADS * VT_ROWS, tq), F32)],
        compiler_params=pltpu.CompilerParams(
            dimension_semantics=("parallel", "arbitrary"), vmem_limit_bytes=VMEM_LIMIT),
        name="mla_attention",
    )(qt, k, vt, qn, kn)


def _mix_kernel(x_ref, attn_ref, u_ref, up_ref, un_ref, ga_ref, gb_ref,
                w_oa_ref, w_pg_ref, pscale_ref, w_ob_ref, w_out_ref, g_post_ref,
                o_ref, ext_ref, *, seq_len):
    i = pl.program_id(0)
    tm = u_ref.shape[0]
    ext_ref[0:HALO, :] = jnp.where(i > 0, up_ref[...], 0.0)
    ext_ref[HALO:HALO + tm, :] = u_ref[...]
    ext_ref[HALO + tm:, :] = jnp.where(i < pl.num_programs(0) - 1, un_ref[...], 0.0)

    t = i * tm + lax.broadcasted_iota(jnp.int32, (tm, 1), 0)
    mixed = []
    for g, w in enumerate(POOL_WINDOWS):
        left = w // 2
        right = w - left - 1
        cols = slice(g * POOL_GROUP_DIM, (g + 1) * POOL_GROUP_DIM)
        win = ext_ref[HALO - left:HALO - left + tm, cols]
        for d in range(-left + 1, right + 1):
            win = win + ext_ref[HALO + d:HALO + d + tm, cols]
        cnt = (jnp.minimum(t + right, seq_len - 1) - jnp.maximum(t - left, 0) + 1).astype(F32)
        pooled = win / cnt - u_ref[:, cols]
        mixed.append(_dot(pooled.astype(BF16), w_pg_ref[g]))
    pool = jnp.concatenate(mixed, axis=1) * pscale_ref[...]

    a = _dot(attn_ref[...], w_oa_ref[...])
    b = _dot(pool.astype(BF16), w_ob_ref[...])
    merged = ga_ref[...] * a + gb_ref[...] * b
    y = _dot(merged.astype(BF16), w_out_ref[...])
    o_ref[...] = x_ref[...] + _rms(y, g_post_ref[...])


def _mix_call(x, attn, u, ga, gb, w_oa, w_pg, pscale, w_ob, w_out, g_post):
    s = x.shape[0]
    tm = MIX_TM
    nb = tm // HALO
    last = s // HALO - 1
    row = lambda w: pl.BlockSpec((tm, w), lambda i: (i, 0))
    full = lambda a: pl.BlockSpec(a.shape, lambda i: (0,) * a.ndim)
    prev = pl.BlockSpec((HALO, POOL_WIDTH), lambda i: (jnp.maximum(i * nb - 1, 0), 0))
    nxt = pl.BlockSpec((HALO, POOL_WIDTH), lambda i: (jnp.minimum((i + 1) * nb, last), 0))
    return pl.pallas_call(
        functools.partial(_mix_kernel, seq_len=s),
        grid=(s // tm,),
        in_specs=[row(D_MODEL), row(POOL_WIDTH), row(POOL_WIDTH), prev, nxt,
                  row(D_MODEL), row(D_MODEL),
                  full(w_oa), full(w_pg), full(pscale), full(w_ob), full(w_out), full(g_post)],
        out_specs=row(D_MODEL),
        out_shape=jax.ShapeDtypeStruct((s, D_MODEL), F32),
        scratch_shapes=[pltpu.VMEM((tm + 2 * HALO, POOL_WIDTH), F32)],
        compiler_params=pltpu.CompilerParams(
            dimension_semantics=("parallel",), vmem_limit_bytes=VMEM_LIMIT),
        name="mix_merge",
    )(x, attn, u, u, u, ga, gb, w_oa, w_pg, pscale, w_ob, w_out, g_post)


def _ffn_kernel(x_ref, g_pre_ref, w_gu_ref, w_dn_ref, g_post_ref, o_ref):
    x = x_ref[...]
    hf = _rms(x, g_pre_ref[...]).astype(BF16)
    gu = _dot(hf, w_gu_ref[...])
    act = jax.nn.silu(gu[:, :D_FF]) * gu[:, D_FF:]
    ff = _dot(act.astype(BF16), w_dn_ref[...])
    o_ref[...] = x + _rms(ff, g_post_ref[...])


def _ffn_call(x, g_pre, w_gu, w_dn, g_post):
    s = x.shape[0]
    tm = FFN_TM
    row = pl.BlockSpec((tm, D_MODEL), lambda i: (i, 0))
    full = lambda a: pl.BlockSpec(a.shape, lambda i: (0,) * a.ndim)
    return pl.pallas_call(
        _ffn_kernel,
        grid=(s // tm,),
        in_specs=[row, full(g_pre), full(w_gu), full(w_dn), full(g_post)],
        out_specs=row,
        out_shape=jax.ShapeDtypeStruct((s, D_MODEL), F32),
        compiler_params=pltpu.CompilerParams(
            dimension_semantics=("parallel",), vmem_limit_bytes=VMEM_LIMIT),
        name="swiglu_ffn",
    )(x, g_pre, w_gu, w_dn, g_post)


def _head_tiles(w, per_head, keep):
    r = w.shape[0]
    w = w.reshape(r, N_HEADS, per_head)[:, :, :keep]
    return jnp.pad(w, ((0, 0), (0, 0), (0, LANES - keep))).reshape(r, HEAD_W)


def kernel(x, positions, g_mix_pre, w_in, g_q_lat, w_uq, g_kv_lat, w_ukv, w_o_attn,
           w_pool_group, pool_scale, w_o_pool, w_out, g_mix_post, g_ffn_pre,
           w_gate_up, w_down, g_ffn_post):
    b, s, d = x.shape
    assert b == 1 and d == D_MODEL
    assert s % max(PRE_TM, ATTN_TQ, ATTN_TK, MIX_TM, FFN_TM) == 0
    vec = lambda g: g.reshape(1, -1).astype(F32)

    o_kr = Q_LORA_RANK + KV_LORA_RANK
    zeros = lambda n: jnp.zeros((D_MODEL, n), w_in.dtype)
    w_in_p = jnp.concatenate(
        [w_in[:, :o_kr], zeros(QK_NOPE_DIM), w_in[:, o_kr:o_kr + QK_ROPE_DIM],
         zeros(LANES - QK_HEAD_DIM), w_in[:, o_kr + QK_ROPE_DIM:]], axis=1).astype(BF16)
    w_uq_p = _head_tiles(w_uq, QK_HEAD_DIM, QK_HEAD_DIM).astype(BF16)
    w_uk_p = _head_tiles(w_ukv, QK_NOPE_DIM + V_HEAD_DIM, QK_NOPE_DIM).astype(BF16)
    w_uv = w_ukv.reshape(KV_LORA_RANK, N_HEADS, QK_NOPE_DIM + V_HEAD_DIM)[:, :, QK_NOPE_DIM:]
    w_uv = w_uv.reshape(KV_LORA_RANK, N_HEADS * V_HEAD_DIM).astype(BF16)

    half = QK_ROPE_DIM // 2
    inv = 1.0 / (ROPE_BASE ** (jnp.arange(0, QK_ROPE_DIM, 2, dtype=F32) / QK_ROPE_DIM))
    inv_lane = jnp.concatenate(
        [jnp.zeros((QK_NOPE_DIM,), F32), inv, inv, jnp.zeros((LANES - QK_HEAD_DIM,), F32)])
    assert inv.shape == (half,)

    x2 = x.reshape(s, d)
    pos = positions.reshape(s, 1)
    q, k, v, u, ga, gb, qn, kn = _pre_call(
        x2, pos, vec(g_mix_pre), w_in_p, vec(g_q_lat), w_uq_p, vec(g_kv_lat), w_uk_p, w_uv,
        inv_lane.reshape(1, LANES))
    attn = _attn_call(q, k, v, qn, kn)
    x1 = _mix_call(x2, attn, u, ga, gb, w_o_attn.astype(BF16), w_pool_group.astype(BF16),
                   vec(pool_scale), w_o_pool.astype(BF16), w_out.astype(BF16), vec(g_mix_post))
    out = _ffn_call(x1, vec(g_ffn_pre), w_gate_up.astype(BF16), w_down.astype(BF16),
                    vec(g_ffn_post))
    return out.reshape(b, s, d)
```

```python
import functools
import math

import jax
import jax.numpy as jnp
from jax import lax
from jax.experimental import pallas as pl
from jax.experimental.pallas import tpu as pltpu

D_MODEL = 1024
N_HEADS = 8
QK_NOPE_DIM = 64
QK_ROPE_DIM = 32
V_HEAD_DIM = 64
Q_LORA_RANK = 384
KV_LORA_RANK = 256
QK_HEAD_DIM = QK_NOPE_DIM + QK_ROPE_DIM
ROPE_BASE = 10000.0
POOL_WINDOWS = (2, 4, 8, 16)
POOL_GROUP_DIM = 128
POOL_WIDTH = len(POOL_WINDOWS) * POOL_GROUP_DIM
D_FF = 2816
NORM_EPS = 1e-6

LANES = 128
SUBLANES = 8
HALO = 8
HEAD_W = N_HEADS * LANES
VMEM_LIMIT = 56 * 1024 * 1024

Q_SCALE = (QK_HEAD_DIM ** -0.5) * math.log2(math.e)

PRE_TM = 512
ATTN_TQ = 1024
ATTN_TK = 1024
QK_LEAD = 1
PV_LAG = 1
EXP_HEADROOM = 60.0
BF16_ROWS = 16
VT_ROWS = V_HEAD_DIM + BF16_ROWS
MIX_TM = 512
FFN_TM = 256

F32 = jnp.float32
BF16 = jnp.bfloat16


def _rms(x, g):
    return x * lax.rsqrt(jnp.mean(x * x, axis=-1, keepdims=True) + NORM_EPS) * g


def _dot(a, b):
    return jnp.dot(a, b, preferred_element_type=F32)


def _pre_kernel(x_ref, pos_ref, g_pre_ref, w_in_ref, g_q_ref, w_uq_ref, g_kv_ref,
                w_uk_ref, w_uv_ref, inv_ref,
                q_ref, k_ref, v_ref, u_ref, ga_ref, gb_ref, qn_ref, kn_ref):
    h = _rms(x_ref[...], g_pre_ref[...]).astype(BF16)
    p = _dot(h, w_in_ref[...])
    o = 0
    c_q = p[:, o:o + Q_LORA_RANK]; o += Q_LORA_RANK
    c_kv = p[:, o:o + KV_LORA_RANK]; o += KV_LORA_RANK
    kr = p[:, o:o + LANES]; o += LANES
    u_ref[...] = p[:, o:o + POOL_WIDTH]; o += POOL_WIDTH
    ga_ref[...] = jax.nn.sigmoid(p[:, o:o + D_MODEL]); o += D_MODEL
    gb_ref[...] = jax.nn.sigmoid(p[:, o:o + D_MODEL])

    ang = inv_ref[...] * pos_ref[...].astype(F32)

    def to_tile(t):
        tm = t.shape[1]
        pads = (jnp.zeros((QK_NOPE_DIM, tm), F32), jnp.zeros((LANES - QK_HEAD_DIM, tm), F32))
        return jnp.concatenate([pads[0], t, pads[1]], axis=0).T

    cos = to_tile(jnp.cos(ang))
    sin = to_tile(jnp.sin(ang))
    lane = lax.broadcasted_iota(jnp.int32, (1, LANES), 1)
    half = QK_ROPE_DIM // 2
    first = (lane >= QK_NOPE_DIM) & (lane < QK_NOPE_DIM + half)
    second = (lane >= QK_NOPE_DIM + half) & (lane < QK_HEAD_DIM)
    t_c = jnp.where(lane < QK_NOPE_DIM, 1.0, jnp.where(lane < QK_HEAD_DIM, cos, 0.0))
    t_m = jnp.where(first, -sin, 0.0)
    t_p = jnp.where(second, sin, 0.0)

    def rope(t):
        ahead = pltpu.roll(t, LANES - half, axis=1)
        behind = pltpu.roll(t, half, axis=1)
        return t * t_c + ahead * t_m + behind * t_p

    q = _dot(_rms(c_q, g_q_ref[...]).astype(BF16), w_uq_ref[...])
    ckv_n = _rms(c_kv, g_kv_ref[...]).astype(BF16)
    k = _dot(ckv_n, w_uk_ref[...])
    v = _dot(ckv_n, w_uv_ref[...])
    kr = rope(kr)
    for hd in range(N_HEADS):
        sl = slice(hd * LANES, (hd + 1) * LANES)
        qt = (rope(q[:, sl]) * Q_SCALE).T.astype(BF16)
        kb = (k[:, sl] + kr).astype(BF16)
        q_ref[hd] = qt
        k_ref[hd] = kb
        qf = qt.astype(F32)
        kf = kb.astype(F32)
        qn_ref[hd:hd + 1, :] = jnp.sum(qf * qf, axis=0, keepdims=True)
        k_sq = jnp.max(jnp.sum(kf * kf, axis=1, keepdims=True), axis=0, keepdims=True)
        kn_ref[0, hd:hd + 1, :] = jnp.broadcast_to(k_sq, (1, LANES))
    vt = v.T
    pad_row = lax.broadcasted_iota(jnp.int32, (VT_ROWS - V_HEAD_DIM, vt.shape[1]), 0)
    ones_pad = jnp.where(pad_row == 0, 1.0, 0.0).astype(BF16)
    for hd in range(N_HEADS):
        lo = hd * VT_ROWS
        v_ref[lo:lo + V_HEAD_DIM, :] = vt[hd * V_HEAD_DIM:(hd + 1) * V_HEAD_DIM, :].astype(BF16)
        v_ref[lo + V_HEAD_DIM:lo + VT_ROWS, :] = ones_pad


def _pre_call(x, pos, g_pre, w_in_p, g_q, w_uq_p, g_kv, w_uk_p, w_uv, inv_rows):
    s = x.shape[0]
    tm = PRE_TM
    row = lambda w: pl.BlockSpec((tm, w), lambda i: (i, 0))
    full = lambda a: pl.BlockSpec(a.shape, lambda i: (0,) * a.ndim)
    heads = lambda n: pl.BlockSpec((n, tm, LANES), lambda i: (0, i, 0))
    return pl.pallas_call(
        _pre_kernel,
        grid=(s // tm,),
        in_specs=[row(D_MODEL), pl.BlockSpec((1, tm), lambda i: (0, i)), full(g_pre),
                  full(w_in_p), full(g_q), full(w_uq_p), full(g_kv), full(w_uk_p), full(w_uv),
                  full(inv_rows)],
        out_specs=[pl.BlockSpec((N_HEADS, LANES, tm), lambda i: (0, 0, i)), heads(N_HEADS),
                   pl.BlockSpec((N_HEADS * VT_ROWS, tm), lambda i: (0, i)),
                   row(POOL_WIDTH), row(D_MODEL), row(D_MODEL),
                   pl.BlockSpec((N_HEADS, tm), lambda i: (0, i)),
                   pl.BlockSpec((1, N_HEADS, LANES), lambda i: (i, 0, 0))],
        out_shape=[jax.ShapeDtypeStruct((N_HEADS, LANES, s), BF16),
                   jax.ShapeDtypeStruct((N_HEADS, s, LANES), BF16),
                   jax.ShapeDtypeStruct((N_HEADS * VT_ROWS, s), BF16),
                   jax.ShapeDtypeStruct((s, POOL_WIDTH), F32),
                   jax.ShapeDtypeStruct((s, D_MODEL), F32),
                   jax.ShapeDtypeStruct((s, D_MODEL), F32),
                   jax.ShapeDtypeStruct((N_HEADS, s), F32),
                   jax.ShapeDtypeStruct((s // tm, N_HEADS, LANES), F32)],
        compiler_params=pltpu.CompilerParams(
            dimension_semantics=("parallel",), vmem_limit_bytes=VMEM_LIMIT),
        name="pre_proj",
    )(x, pos, g_pre, w_in_p, g_q, w_uq_p, g_kv, w_uk_p, w_uv, inv_rows)


def _attn_kernel(qt_ref, k_ref, vt_ref, qn_ref, kn_ref, kn_next_ref, o_ref, m_ref, acc_ref, flag_ref):
    j = pl.program_id(1)

    def head_rows(hd):
        return slice(hd * VT_ROWS, (hd + 1) * VT_ROWS)

    def within_headroom(kn_blk_ref):
        k_norm = jnp.sqrt(jnp.max(kn_blk_ref[...], axis=0)[:, :1])
        bound = jnp.sqrt(qn_ref[...]) * k_norm
        excess = jnp.max(bound - m_ref[...].reshape(bound.shape))
        return (excess <= EXP_HEADROOM).astype(jnp.int32)

    @pl.when(j == 0)
    def _init():
        for hd in range(N_HEADS):
            m_ref[hd] = _dot(k_ref[hd, 0:BF16_ROWS, :], qt_ref[hd])[0:1, :]
        acc_ref[...] = jnp.zeros(acc_ref.shape, F32)
        flag_ref[0] = within_headroom(kn_ref)

    streaming = flag_ref[0] == 1

    @pl.when(streaming)
    def _stream():
        flag_ref[0] = within_headroom(kn_next_ref)
        scores = {}
        for t in range(N_HEADS + QK_LEAD):
            if t < N_HEADS:
                scores[t] = _dot(k_ref[t], qt_ref[t])
            hd = t - QK_LEAD
            if hd >= 0:
                p = jnp.exp2(scores.pop(hd) - m_ref[hd]).astype(BF16)
                rows = head_rows(hd)
                acc_ref[rows, :] = acc_ref[rows, :] + _dot(vt_ref[rows, :], p)

    @pl.when(jnp.logical_not(streaming))
    def _two_pass():
        scores, probs, alphas = {}, {}, {}
        for t in range(N_HEADS + QK_LEAD + PV_LAG):
            if t < N_HEADS:
                scores[t] = _dot(k_ref[t], qt_ref[t])
            hd = t - QK_LEAD
            if 0 <= hd < N_HEADS:
                s = scores.pop(hd)
                m_prev = m_ref[hd]
                m_new = jnp.maximum(m_prev, jnp.max(s, axis=0, keepdims=True))
                alphas[hd] = jnp.exp2(m_prev - m_new)
                probs[hd] = jnp.exp2(s - m_new).astype(BF16)
                m_ref[hd] = m_new
            hd = t - QK_LEAD - PV_LAG
            if 0 <= hd < N_HEADS:
                rows = head_rows(hd)
                pv = _dot(vt_ref[rows, :], probs.pop(hd))
                acc_ref[rows, :] = alphas.pop(hd) * acc_ref[rows, :] + pv
        flag_ref[0] = within_headroom(kn_next_ref)

    @pl.when(j == pl.num_programs(1) - 1)
    def _finish():
        outs = []
        for hd in range(N_HEADS):
            lo = hd * VT_ROWS
            denom = acc_ref[lo + V_HEAD_DIM:lo + V_HEAD_DIM + 1, :]
            outs.append(acc_ref[lo:lo + V_HEAD_DIM, :] / denom)
        o_ref[...] = jnp.concatenate(outs, axis=0).T.astype(o_ref.dtype)


def _attn_call(qt, k, vt, qn, kn):
    s = k.shape[1]
    tq, tk = ATTN_TQ, ATTN_TK
    width = N_HEADS * V_HEAD_DIM
    kn_rows = tk // PRE_TM
    last_kv = s // tk - 1
    return pl.pallas_call(
        _attn_kernel,
        grid=(s // tq, s // tk),
        in_specs=[pl.BlockSpec((N_HEADS, LANES, tq), lambda i, j: (0, 0, i)),
                  pl.BlockSpec((N_HEADS, tk, LANES), lambda i, j: (0, j, 0)),
                  pl.BlockSpec((N_HEADS * VT_ROWS, tk), lambda i, j: (0, j)),
                  pl.BlockSpec((N_HEADS, tq), lambda i, j: (0, i)),
                  pl.BlockSpec((kn_rows, N_HEADS, LANES), lambda i, j: (j, 0, 0)),
                  pl.BlockSpec((kn_rows, N_HEADS, LANES),
                               lambda i, j: (jnp.minimum(j + 1, last_kv), 0, 0))],
        out_specs=pl.BlockSpec((tq, width), lambda i, j: (i, 0)),
        out_shape=jax.ShapeDtypeStruct((s, width), BF16),
        scratch_shapes=[pltpu.VMEM((N_HEADS, 1, tq), F32),
                        pltpu.VMEM((N_HEADS * VT_ROWS, tq), F32),
                        pltpu.SMEM((1,), jnp.int32)],
        compiler_params=pltpu.CompilerParams(
            dimension_semantics=("parallel", "arbitrary"), vmem_limit_bytes=VMEM_LIMIT),
        name="mla_attention",
    )(qt, k, vt, qn, kn, kn)


def _mix_kernel(x_ref, attn_ref, u_ref, up_ref, un_ref, ga_ref, gb_ref,
                w_oa_ref, w_pg_ref, pscale_ref, w_ob_ref, w_out_ref, g_post_ref,
                o_ref, ext_ref, *, seq_len):
    i = pl.program_id(0)
    tm = u_ref.shape[0]
    ext_ref[0:HALO, :] = jnp.where(i > 0, up_ref[...], 0.0)
    ext_ref[HALO:HALO + tm, :] = u_ref[...]
    ext_ref[HALO + tm:, :] = jnp.where(i < pl.num_programs(0) - 1, un_ref[...], 0.0)

    t = i * tm + lax.broadcasted_iota(jnp.int32, (tm, 1), 0)
    mixed = []
    for g, w in enumerate(POOL_WINDOWS):
        left = w // 2
        right = w - left - 1
        cols = slice(g * POOL_GROUP_DIM, (g + 1) * POOL_GROUP_DIM)
        win = ext_ref[HALO - left:HALO - left + tm, cols]
        for d in range(-left + 1, right + 1):
            win = win + ext_ref[HALO + d:HALO + d + tm, cols]
        cnt = (jnp.minimum(t + right, seq_len - 1) - jnp.maximum(t - left, 0) + 1).astype(F32)
        pooled = win / cnt - u_ref[:, cols]
        mixed.append(_dot(pooled.astype(BF16), w_pg_ref[g]))
    pool = jnp.concatenate(mixed, axis=1) * pscale_ref[...]

    a = _dot(attn_ref[...], w_oa_ref[...])
    b = _dot(pool.astype(BF16), w_ob_ref[...])
    merged = ga_ref[...] * a + gb_ref[...] * b
    y = _dot(merged.astype(BF16), w_out_ref[...])
    o_ref[...] = x_ref[...] + _rms(y, g_post_ref[...])


def _mix_call(x, attn, u, ga, gb, w_oa, w_pg, pscale, w_ob, w_out, g_post):
    s = x.shape[0]
    tm = MIX_TM
    nb = tm // HALO
    last = s // HALO - 1
    row = lambda w: pl.BlockSpec((tm, w), lambda i: (i, 0))
    full = lambda a: pl.BlockSpec(a.shape, lambda i: (0,) * a.ndim)
    prev = pl.BlockSpec((HALO, POOL_WIDTH), lambda i: (jnp.maximum(i * nb - 1, 0), 0))
    nxt = pl.BlockSpec((HALO, POOL_WIDTH), lambda i: (jnp.minimum((i + 1) * nb, last), 0))
    return pl.pallas_call(
        functools.partial(_mix_kernel, seq_len=s),
        grid=(s // tm,),
        in_specs=[row(D_MODEL), row(POOL_WIDTH), row(POOL_WIDTH), prev, nxt,
                  row(D_MODEL), row(D_MODEL),
                  full(w_oa), full(w_pg), full(pscale), full(w_ob), full(w_out), full(g_post)],
        out_specs=row(D_MODEL),
        out_shape=jax.ShapeDtypeStruct((s, D_MODEL), F32),
        scratch_shapes=[pltpu.VMEM((tm + 2 * HALO, POOL_WIDTH), F32)],
        compiler_params=pltpu.CompilerParams(
            dimension_semantics=("parallel",), vmem_limit_bytes=VMEM_LIMIT),
        name="mix_merge",
    )(x, attn, u, u, u, ga, gb, w_oa, w_pg, pscale, w_ob, w_out, g_post)


def _ffn_kernel(x_ref, g_pre_ref, w_gu_ref, w_dn_ref, g_post_ref, o_ref):
    x = x_ref[...]
    hf = _rms(x, g_pre_ref[...]).astype(BF16)
    gu = _dot(hf, w_gu_ref[...])
    act = jax.nn.silu(gu[:, :D_FF]) * gu[:, D_FF:]
    ff = _dot(act.astype(BF16), w_dn_ref[...])
    o_ref[...] = x + _rms(ff, g_post_ref[...])


def _ffn_call(x, g_pre, w_gu, w_dn, g_post):
    s = x.shape[0]
    tm = FFN_TM
    row = pl.BlockSpec((tm, D_MODEL), lambda i: (i, 0))
    full = lambda a: pl.BlockSpec(a.shape, lambda i: (0,) * a.ndim)
    return pl.pallas_call(
        _ffn_kernel,
        grid=(s // tm,),
        in_specs=[row, full(g_pre), full(w_gu), full(w_dn), full(g_post)],
        out_specs=row,
        out_shape=jax.ShapeDtypeStruct((s, D_MODEL), F32),
        compiler_params=pltpu.CompilerParams(
            dimension_semantics=("parallel",), vmem_limit_bytes=VMEM_LIMIT),
        name="swiglu_ffn",
    )(x, g_pre, w_gu, w_dn, g_post)


def _head_tiles(w, per_head, keep):
    r = w.shape[0]
    w = w.reshape(r, N_HEADS, per_head)[:, :, :keep]
    return jnp.pad(w, ((0, 0), (0, 0), (0, LANES - keep))).reshape(r, HEAD_W)


def kernel(x, positions, g_mix_pre, w_in, g_q_lat, w_uq, g_kv_lat, w_ukv, w_o_attn,
           w_pool_group, pool_scale, w_o_pool, w_out, g_mix_post, g_ffn_pre,
           w_gate_up, w_down, g_ffn_post):
    b, s, d = x.shape
    assert b == 1 and d == D_MODEL
    assert s % max(PRE_TM, ATTN_TQ, ATTN_TK, MIX_TM, FFN_TM) == 0
    vec = lambda g: g.reshape(1, -1).astype(F32)

    o_kr = Q_LORA_RANK + KV_LORA_RANK
    zeros = lambda n: jnp.zeros((D_MODEL, n), w_in.dtype)
    w_in_p = jnp.concatenate(
        [w_in[:, :o_kr], zeros(QK_NOPE_DIM), w_in[:, o_kr:o_kr + QK_ROPE_DIM],
         zeros(LANES - QK_HEAD_DIM), w_in[:, o_kr + QK_ROPE_DIM:]], axis=1).astype(BF16)
    w_uq_p = _head_tiles(w_uq, QK_HEAD_DIM, QK_HEAD_DIM).astype(BF16)
    w_uk_p = _head_tiles(w_ukv, QK_NOPE_DIM + V_HEAD_DIM, QK_NOPE_DIM).astype(BF16)
    w_uv = w_ukv.reshape(KV_LORA_RANK, N_HEADS, QK_NOPE_DIM + V_HEAD_DIM)[:, :, QK_NOPE_DIM:]
    w_uv = w_uv.reshape(KV_LORA_RANK, N_HEADS * V_HEAD_DIM).astype(BF16)

    inv = 1.0 / (ROPE_BASE ** (jnp.arange(0, QK_ROPE_DIM, 2, dtype=F32) / QK_ROPE_DIM))
    inv_rows = jnp.concatenate([inv, inv]).reshape(QK_ROPE_DIM, 1)

    x2 = x.reshape(s, d)
    q, k, v, u, ga, gb, qn, kn = _pre_call(
        x2, positions.reshape(1, s), vec(g_mix_pre), w_in_p, vec(g_q_lat), w_uq_p,
        vec(g_kv_lat), w_uk_p, w_uv, inv_rows)
    attn = _attn_call(q, k, v, qn, kn)
    x1 = _mix_call(x2, attn, u, ga, gb, w_o_attn.astype(BF16), w_pool_group.astype(BF16),
                   vec(pool_scale), w_o_pool.astype(BF16), w_out.astype(BF16), vec(g_mix_post))
    out = _ffn_call(x1, vec(g_ffn_pre), w_gate_up.astype(BF16), w_down.astype(BF16),
                    vec(g_ffn_post))
    return out.reshape(b, s, d)
```

```python
import functools
import math

import jax
import jax.numpy as jnp
from jax import lax
from jax.experimental import pallas as pl
from jax.experimental.pallas import tpu as pltpu

D_MODEL = 1024
N_HEADS = 8
QK_NOPE_DIM = 64
QK_ROPE_DIM = 32
V_HEAD_DIM = 64
Q_LORA_RANK = 384
KV_LORA_RANK = 256
QK_HEAD_DIM = QK_NOPE_DIM + QK_ROPE_DIM
ROPE_BASE = 10000.0
POOL_WINDOWS = (2, 4, 8, 16)
POOL_GROUP_DIM = 128
POOL_WIDTH = len(POOL_WINDOWS) * POOL_GROUP_DIM
D_FF = 2816
NORM_EPS = 1e-6

LANES = 128
SUBLANES = 8
HALO = 8
HEAD_W = N_HEADS * LANES
VMEM_LIMIT = 56 * 1024 * 1024

Q_SCALE = (QK_HEAD_DIM ** -0.5) * math.log2(math.e)

PRE_TM = 512
ATTN_TQ = 1024
ATTN_TK = 2048
QK_LEAD = 1
PV_LAG = 1
EXP_HEADROOM = 60.0
BF16_ROWS = 16
VT_ROWS = V_HEAD_DIM + BF16_ROWS
MIX_TM = 512
FFN_TM = 256

F32 = jnp.float32
BF16 = jnp.bfloat16


def _rms(x, g):
    return x * lax.rsqrt(jnp.mean(x * x, axis=-1, keepdims=True) + NORM_EPS) * g


def _dot(a, b):
    return jnp.dot(a, b, preferred_element_type=F32)


def _pre_kernel(x_ref, pos_ref, g_pre_ref, w_in_ref, g_q_ref, w_uq_ref, g_kv_ref,
                w_uk_ref, w_uv_ref, inv_ref,
                q_ref, k_ref, v_ref, u_ref, ga_ref, gb_ref, qn_ref, kn_ref):
    h = _rms(x_ref[...], g_pre_ref[...]).astype(BF16)
    p = _dot(h, w_in_ref[...])
    o = 0
    c_q = p[:, o:o + Q_LORA_RANK]; o += Q_LORA_RANK
    c_kv = p[:, o:o + KV_LORA_RANK]; o += KV_LORA_RANK
    kr = p[:, o:o + LANES]; o += LANES
    u_ref[...] = p[:, o:o + POOL_WIDTH]; o += POOL_WIDTH
    ga_ref[...] = jax.nn.sigmoid(p[:, o:o + D_MODEL]); o += D_MODEL
    gb_ref[...] = jax.nn.sigmoid(p[:, o:o + D_MODEL])

    ang = inv_ref[...] * pos_ref[...].astype(F32)

    def to_tile(t):
        tm = t.shape[1]
        pads = (jnp.zeros((QK_NOPE_DIM, tm), F32), jnp.zeros((LANES - QK_HEAD_DIM, tm), F32))
        return jnp.concatenate([pads[0], t, pads[1]], axis=0).T

    cos = to_tile(jnp.cos(ang))
    sin = to_tile(jnp.sin(ang))
    lane = lax.broadcasted_iota(jnp.int32, (1, LANES), 1)
    half = QK_ROPE_DIM // 2
    first = (lane >= QK_NOPE_DIM) & (lane < QK_NOPE_DIM + half)
    second = (lane >= QK_NOPE_DIM + half) & (lane < QK_HEAD_DIM)
    t_c = jnp.where(lane < QK_NOPE_DIM, 1.0, jnp.where(lane < QK_HEAD_DIM, cos, 0.0))
    t_m = jnp.where(first, -sin, 0.0)
    t_p = jnp.where(second, sin, 0.0)

    def rope(t):
        ahead = pltpu.roll(t, LANES - half, axis=1)
        behind = pltpu.roll(t, half, axis=1)
        return t * t_c + ahead * t_m + behind * t_p

    q = _dot(_rms(c_q, g_q_ref[...]).astype(BF16), w_uq_ref[...])
    ckv_n = _rms(c_kv, g_kv_ref[...]).astype(BF16)
    k = _dot(ckv_n, w_uk_ref[...])
    v = _dot(ckv_n, w_uv_ref[...])
    kr = rope(kr)
    for hd in range(N_HEADS):
        sl = slice(hd * LANES, (hd + 1) * LANES)
        qt = (rope(q[:, sl]) * Q_SCALE).T.astype(BF16)
        kb = (k[:, sl] + kr).astype(BF16)
        q_ref[hd] = qt
        k_ref[hd] = kb
        qf = qt.astype(F32)
        kf = kb.astype(F32)
        qn_ref[hd:hd + 1, :] = jnp.sum(qf * qf, axis=0, keepdims=True)
        k_sq = jnp.max(jnp.sum(kf * kf, axis=1, keepdims=True), axis=0, keepdims=True)
        kn_ref[0, hd:hd + 1, :] = jnp.broadcast_to(k_sq, (1, LANES))
    vt = v.T
    pad_row = lax.broadcasted_iota(jnp.int32, (VT_ROWS - V_HEAD_DIM, vt.shape[1]), 0)
    ones_pad = jnp.where(pad_row == 0, 1.0, 0.0).astype(BF16)
    for hd in range(N_HEADS):
        lo = hd * VT_ROWS
        v_ref[lo:lo + V_HEAD_DIM, :] = vt[hd * V_HEAD_DIM:(hd + 1) * V_HEAD_DIM, :].astype(BF16)
        v_ref[lo + V_HEAD_DIM:lo + VT_ROWS, :] = ones_pad


def _pre_call(x, pos, g_pre, w_in_p, g_q, w_uq_p, g_kv, w_uk_p, w_uv, inv_rows):
    s = x.shape[0]
    tm = PRE_TM
    row = lambda w: pl.BlockSpec((tm, w), lambda i: (i, 0))
    full = lambda a: pl.BlockSpec(a.shape, lambda i: (0,) * a.ndim)
    heads = lambda n: pl.BlockSpec((n, tm, LANES), lambda i: (0, i, 0))
    return pl.pallas_call(
        _pre_kernel,
        grid=(s // tm,),
        in_specs=[row(D_MODEL), pl.BlockSpec((1, tm), lambda i: (0, i)), full(g_pre),
                  full(w_in_p), full(g_q), full(w_uq_p), full(g_kv), full(w_uk_p), full(w_uv),
                  full(inv_rows)],
        out_specs=[pl.BlockSpec((N_HEADS, LANES, tm), lambda i: (0, 0, i)), heads(N_HEADS),
                   pl.BlockSpec((N_HEADS * VT_ROWS, tm), lambda i: (0, i)),
                   row(POOL_WIDTH), row(D_MODEL), row(D_MODEL),
                   pl.BlockSpec((N_HEADS, tm), lambda i: (0, i)),
                   pl.BlockSpec((1, N_HEADS, LANES), lambda i: (i, 0, 0))],
        out_shape=[jax.ShapeDtypeStruct((N_HEADS, LANES, s), BF16),
                   jax.ShapeDtypeStruct((N_HEADS, s, LANES), BF16),
                   jax.ShapeDtypeStruct((N_HEADS * VT_ROWS, s), BF16),
                   jax.ShapeDtypeStruct((s, POOL_WIDTH), F32),
                   jax.ShapeDtypeStruct((s, D_MODEL), F32),
                   jax.ShapeDtypeStruct((s, D_MODEL), F32),
                   jax.ShapeDtypeStruct((N_HEADS, s), F32),
                   jax.ShapeDtypeStruct((s // tm, N_HEADS, LANES), F32)],
        compiler_params=pltpu.CompilerParams(
            dimension_semantics=("parallel",), vmem_limit_bytes=VMEM_LIMIT),
        name="pre_proj",
    )(x, pos, g_pre, w_in_p, g_q, w_uq_p, g_kv, w_uk_p, w_uv, inv_rows)


def _attn_kernel(qt_ref, k_ref, vt_ref, qn_ref, kn_ref, kn_next_ref, o_ref, m_ref, acc_ref, flag_ref):
    j = pl.program_id(1)

    def head_rows(hd):
        return slice(hd * VT_ROWS, (hd + 1) * VT_ROWS)

    def within_headroom(kn_blk_ref):
        k_norm = jnp.sqrt(jnp.max(kn_blk_ref[...], axis=0)[:, :1])
        bound = jnp.sqrt(qn_ref[...]) * k_norm
        excess = jnp.max(bound - m_ref[...].reshape(bound.shape))
        return (excess <= EXP_HEADROOM).astype(jnp.int32)

    @pl.when(j == 0)
    def _init():
        for hd in range(N_HEADS):
            m_ref[hd] = _dot(k_ref[hd, 0:BF16_ROWS, :], qt_ref[hd])[0:1, :]
        acc_ref[...] = jnp.zeros(acc_ref.shape, F32)
        flag_ref[0] = within_headroom(kn_ref)

    streaming = flag_ref[0] == 1

    @pl.when(streaming)
    def _stream():
        flag_ref[0] = within_headroom(kn_next_ref)
        scores = {}
        for t in range(N_HEADS + QK_LEAD):
            if t < N_HEADS:
                scores[t] = _dot(k_ref[t], qt_ref[t])
            hd = t - QK_LEAD
            if hd >= 0:
                p = jnp.exp2(scores.pop(hd) - m_ref[hd]).astype(BF16)
                rows = head_rows(hd)
                acc_ref[rows, :] = acc_ref[rows, :] + _dot(vt_ref[rows, :], p)

    @pl.when(jnp.logical_not(streaming))
    def _two_pass():
        scores, probs, alphas = {}, {}, {}
        for t in range(N_HEADS + QK_LEAD + PV_LAG):
            if t < N_HEADS:
                scores[t] = _dot(k_ref[t], qt_ref[t])
            hd = t - QK_LEAD
            if 0 <= hd < N_HEADS:
                s = scores.pop(hd)
                m_prev = m_ref[hd]
                m_new = jnp.maximum(m_prev, jnp.max(s, axis=0, keepdims=True))
                alphas[hd] = jnp.exp2(m_prev - m_new)
                probs[hd] = jnp.exp2(s - m_new).astype(BF16)
                m_ref[hd] = m_new
            hd = t - QK_LEAD - PV_LAG
            if 0 <= hd < N_HEADS:
                rows = head_rows(hd)
                pv = _dot(vt_ref[rows, :], probs.pop(hd))
                acc_ref[rows, :] = alphas.pop(hd) * acc_ref[rows, :] + pv
        flag_ref[0] = within_headroom(kn_next_ref)

    @pl.when(j == pl.num_programs(1) - 1)
    def _finish():
        outs = []
        for hd in range(N_HEADS):
            lo = hd * VT_ROWS
            denom = acc_ref[lo + V_HEAD_DIM:lo + V_HEAD_DIM + 1, :]
            outs.append(acc_ref[lo:lo + V_HEAD_DIM, :] / denom)
        o_ref[...] = jnp.concatenate(outs, axis=0).T.astype(o_ref.dtype)


def _attn_call(qt, k, vt, qn, kn):
    s = k.shape[1]
    tq, tk = ATTN_TQ, ATTN_TK
    width = N_HEADS * V_HEAD_DIM
    kn_rows = tk // PRE_TM
    last_kv = s // tk - 1
    return pl.pallas_call(
        _attn_kernel,
        grid=(s // tq, s // tk),
        in_specs=[pl.BlockSpec((N_HEADS, LANES, tq), lambda i, j: (0, 0, i)),
                  pl.BlockSpec((N_HEADS, tk, LANES), lambda i, j: (0, j, 0)),
                  pl.BlockSpec((N_HEADS * VT_ROWS, tk), lambda i, j: (0, j)),
                  pl.BlockSpec((N_HEADS, tq), lambda i, j: (0, i)),
                  pl.BlockSpec((kn_rows, N_HEADS, LANES), lambda i, j: (j, 0, 0)),
                  pl.BlockSpec((kn_rows, N_HEADS, LANES),
                               lambda i, j: (jnp.minimum(j + 1, last_kv), 0, 0))],
        out_specs=pl.BlockSpec((tq, width), lambda i, j: (i, 0)),
        out_shape=jax.ShapeDtypeStruct((s, width), BF16),
        scratch_shapes=[pltpu.VMEM((N_HEADS, 1, tq), F32),
                        pltpu.VMEM((N_HEADS * VT_ROWS, tq), F32),
                        pltpu.SMEM((1,), jnp.int32)],
        compiler_params=pltpu.CompilerParams(
            dimension_semantics=("parallel", "arbitrary"), vmem_limit_bytes=VMEM_LIMIT),
        name="mla_attention",
    )(qt, k, vt, qn, kn, kn)


def _mix_kernel(x_ref, attn_ref, u_ref, up_ref, un_ref, ga_ref, gb_ref,
                w_oa_ref, w_pg_ref, pscale_ref, w_ob_ref, w_out_ref, g_post_ref,
                o_ref, ext_ref, *, seq_len):
    i = pl.program_id(0)
    tm = u_ref.shape[0]
    ext_ref[0:HALO, :] = jnp.where(i > 0, up_ref[...], 0.0)
    ext_ref[HALO:HALO + tm, :] = u_ref[...]
    ext_ref[HALO + tm:, :] = jnp.where(i < pl.num_programs(0) - 1, un_ref[...], 0.0)

    t = i * tm + lax.broadcasted_iota(jnp.int32, (tm, 1), 0)
    mixed = []
    for g, w in enumerate(POOL_WINDOWS):
        left = w // 2
        right = w - left - 1
        cols = slice(g * POOL_GROUP_DIM, (g + 1) * POOL_GROUP_DIM)
        win = ext_ref[HALO - left:HALO - left + tm, cols]
        for d in range(-left + 1, right + 1):
            win = win + ext_ref[HALO + d:HALO + d + tm, cols]
        cnt = (jnp.minimum(t + right, seq_len - 1) - jnp.maximum(t - left, 0) + 1).astype(F32)
        pooled = win / cnt - u_ref[:, cols]
        mixed.append(_dot(pooled.astype(BF16), w_pg_ref[g]))
    pool = jnp.concatenate(mixed, axis=1) * pscale_ref[...]

    a = _dot(attn_ref[...], w_oa_ref[...])
    b = _dot(pool.astype(BF16), w_ob_ref[...])
    merged = ga_ref[...] * a + gb_ref[...] * b
    y = _dot(merged.astype(BF16), w_out_ref[...])
    o_ref[...] = x_ref[...] + _rms(y, g_post_ref[...])


def _mix_call(x, attn, u, ga, gb, w_oa, w_pg, pscale, w_ob, w_out, g_post):
    s = x.shape[0]
    tm = MIX_TM
    nb = tm // HALO
    last = s // HALO - 1
    row = lambda w: pl.BlockSpec((tm, w), lambda i: (i, 0))
    full = lambda a: pl.BlockSpec(a.shape, lambda i: (0,) * a.ndim)
    prev = pl.BlockSpec((HALO, POOL_WIDTH), lambda i: (jnp.maximum(i * nb - 1, 0), 0))
    nxt = pl.BlockSpec((HALO, POOL_WIDTH), lambda i: (jnp.minimum((i + 1) * nb, last), 0))
    return pl.pallas_call(
        functools.partial(_mix_kernel, seq_len=s),
        grid=(s // tm,),
        in_specs=[row(D_MODEL), row(POOL_WIDTH), row(POOL_WIDTH), prev, nxt,
                  row(D_MODEL), row(D_MODEL),
                  full(w_oa), full(w_pg), full(pscale), full(w_ob), full(w_out), full(g_post)],
        out_specs=row(D_MODEL),
        out_shape=jax.ShapeDtypeStruct((s, D_MODEL), F32),
        scratch_shapes=[pltpu.VMEM((tm + 2 * HALO, POOL_WIDTH), F32)],
        compiler_params=pltpu.CompilerParams(
            dimension_semantics=("parallel",), vmem_limit_bytes=VMEM_LIMIT),
        name="mix_merge",
    )(x, attn, u, u, u, ga, gb, w_oa, w_pg, pscale, w_ob, w_out, g_post)


def _ffn_kernel(x_ref, g_pre_ref, w_gu_ref, w_dn_ref, g_post_ref, o_ref):
    x = x_ref[...]
    hf = _rms(x, g_pre_ref[...]).astype(BF16)
    gu = _dot(hf, w_gu_ref[...])
    act = jax.nn.silu(gu[:, :D_FF]) * gu[:, D_FF:]
    ff = _dot(act.astype(BF16), w_dn_ref[...])
    o_ref[...] = x + _rms(ff, g_post_ref[...])


def _ffn_call(x, g_pre, w_gu, w_dn, g_post):
    s = x.shape[0]
    tm = FFN_TM
    row = pl.BlockSpec((tm, D_MODEL), lambda i: (i, 0))
    full = lambda a: pl.BlockSpec(a.shape, lambda i: (0,) * a.ndim)
    return pl.pallas_call(
        _ffn_kernel,
        grid=(s // tm,),
        in_specs=[row, full(g_pre), full(w_gu), full(w_dn), full(g_post)],
        out_specs=row,
        out_shape=jax.ShapeDtypeStruct((s, D_MODEL), F32),
        compiler_params=pltpu.CompilerParams(
            dimension_semantics=("parallel",), vmem_limit_bytes=VMEM_LIMIT),
        name="swiglu_ffn",
    )(x, g_pre, w_gu, w_dn, g_post)


def _head_tiles(w, per_head, keep):
    r = w.shape[0]
    w = w.reshape(r, N_HEADS, per_head)[:, :, :keep]
    return jnp.pad(w, ((0, 0), (0, 0), (0, LANES - keep))).reshape(r, HEAD_W)


def kernel(x, positions, g_mix_pre, w_in, g_q_lat, w_uq, g_kv_lat, w_ukv, w_o_attn,
           w_pool_group, pool_scale, w_o_pool, w_out, g_mix_post, g_ffn_pre,
           w_gate_up, w_down, g_ffn_post):
    b, s, d = x.shape
    assert b == 1 and d == D_MODEL
    assert s % max(PRE_TM, ATTN_TQ, ATTN_TK, MIX_TM, FFN_TM) == 0
    vec = lambda g: g.reshape(1, -1).astype(F32)

    o_kr = Q_LORA_RANK + KV_LORA_RANK
    zeros = lambda n: jnp.zeros((D_MODEL, n), w_in.dtype)
    w_in_p = jnp.concatenate(
        [w_in[:, :o_kr], zeros(QK_NOPE_DIM), w_in[:, o_kr:o_kr + QK_ROPE_DIM],
         zeros(LANES - QK_HEAD_DIM), w_in[:, o_kr + QK_ROPE_DIM:]], axis=1).astype(BF16)
    w_uq_p = _head_tiles(w_uq, QK_HEAD_DIM, QK_HEAD_DIM).astype(BF16)
    w_uk_p = _head_tiles(w_ukv, QK_NOPE_DIM + V_HEAD_DIM, QK_NOPE_DIM).astype(BF16)
    w_uv = w_ukv.reshape(KV_LORA_RANK, N_HEADS, QK_NOPE_DIM + V_HEAD_DIM)[:, :, QK_NOPE_DIM:]
    w_uv = w_uv.reshape(KV_LORA_RANK, N_HEADS * V_HEAD_DIM).astype(BF16)

    inv = 1.0 / (ROPE_BASE ** (jnp.arange(0, QK_ROPE_DIM, 2, dtype=F32) / QK_ROPE_DIM))
    inv_rows = jnp.concatenate([inv, inv]).reshape(QK_ROPE_DIM, 1)

    x2 = x.reshape(s, d)
    q, k, v, u, ga, gb, qn, kn = _pre_call(
        x2, positions.reshape(1, s), vec(g_mix_pre), w_in_p, vec(g_q_lat), w_uq_p,
        vec(g_kv_lat), w_uk_p, w_uv, inv_rows)
    attn = _attn_call(q, k, v, qn, kn)
    x1 = _mix_call(x2, attn, u, ga, gb, w_o_attn.astype(BF16), w_pool_group.astype(BF16),
                   vec(pool_scale), w_o_pool.astype(BF16), w_out.astype(BF16), vec(g_mix_post))
    out = _ffn_call(x1, vec(g_ffn_pre), w_gate_up.astype(BF16), w_down.astype(BF16),
                    vec(g_ffn_post))
    return out.reshape(b, s, d)
```

```python
import functools
import math

import jax
import jax.numpy as jnp
from jax import lax
from jax.experimental import pallas as pl
from jax.experimental.pallas import tpu as pltpu

D_MODEL = 1024
N_HEADS = 8
QK_NOPE_DIM = 64
QK_ROPE_DIM = 32
V_HEAD_DIM = 64
Q_LORA_RANK = 384
KV_LORA_RANK = 256
QK_HEAD_DIM = QK_NOPE_DIM + QK_ROPE_DIM
ROPE_BASE = 10000.0
POOL_WINDOWS = (2, 4, 8, 16)
POOL_GROUP_DIM = 128
POOL_WIDTH = len(POOL_WINDOWS) * POOL_GROUP_DIM
D_FF = 2816
NORM_EPS = 1e-6

LANES = 128
SUBLANES = 8
HALO = 8
HEAD_W = N_HEADS * LANES
VMEM_LIMIT = 56 * 1024 * 1024

Q_SCALE = (QK_HEAD_DIM ** -0.5) * math.log2(math.e)

PRE_TM = 512
ATTN_TQ = 1024
ATTN_TK = 2048
QK_LEAD = 1
PV_LAG = 1
EXP_HEADROOM = 60.0
BF16_ROWS = 16
VT_ROWS = V_HEAD_DIM + BF16_ROWS
MIX_TM = 512
FFN_TM = 512

F32 = jnp.float32
BF16 = jnp.bfloat16


def _rms(x, g):
    return x * lax.rsqrt(jnp.mean(x * x, axis=-1, keepdims=True) + NORM_EPS) * g


def _dot(a, b):
    return jnp.dot(a, b, preferred_element_type=F32)


def _pre_kernel(x_ref, pos_ref, g_pre_ref, w_in_ref, g_q_ref, w_uqt_ref, g_kv_ref,
                w_uk_ref, w_uvt_ref, inv_ref,
                q_ref, k_ref, v_ref, u_ref, ga_ref, gb_ref, qn_ref, kn_ref):
    h = _rms(x_ref[...], g_pre_ref[...]).astype(BF16)
    p = _dot(h, w_in_ref[...])
    o = 0
    c_q = p[:, o:o + Q_LORA_RANK]; o += Q_LORA_RANK
    c_kv = p[:, o:o + KV_LORA_RANK]; o += KV_LORA_RANK
    kr = p[:, o:o + LANES]; o += LANES
    u_ref[...] = p[:, o:o + POOL_WIDTH]; o += POOL_WIDTH
    ga_ref[...] = jax.nn.sigmoid(p[:, o:o + D_MODEL]); o += D_MODEL
    gb_ref[...] = jax.nn.sigmoid(p[:, o:o + D_MODEL])

    half = QK_ROPE_DIM // 2
    ang = inv_ref[...] * pos_ref[...].astype(F32)
    cos = jnp.cos(ang)
    sin = jnp.sin(ang)

    def rope_t(t):
        t1 = t[QK_NOPE_DIM:QK_NOPE_DIM + half]
        t2 = t[QK_NOPE_DIM + half:QK_HEAD_DIM]
        return jnp.concatenate(
            [t[:QK_NOPE_DIM], t1 * cos - t2 * sin, t2 * cos + t1 * sin, t[QK_HEAD_DIM:]], axis=0)

    cq_t = _rms(c_q, g_q_ref[...]).T.astype(BF16)
    ckv_n = _rms(c_kv, g_kv_ref[...])
    q_t = _dot(w_uqt_ref[...], cq_t)
    vt = _dot(w_uvt_ref[...], ckv_n.T.astype(BF16))
    k = _dot(ckv_n.astype(BF16), w_uk_ref[...])
    kr = rope_t(kr.T).T
    for hd in range(N_HEADS):
        sl = slice(hd * LANES, (hd + 1) * LANES)
        qt = (rope_t(q_t[sl, :]) * Q_SCALE).astype(BF16)
        kb = (k[:, sl] + kr).astype(BF16)
        q_ref[hd] = qt
        k_ref[hd] = kb
        qf = qt.astype(F32)
        kf = kb.astype(F32)
        qn_ref[hd:hd + 1, :] = jnp.sum(qf * qf, axis=0, keepdims=True)
        k_sq = jnp.max(jnp.sum(kf * kf, axis=1, keepdims=True), axis=0, keepdims=True)
        kn_ref[0, hd:hd + 1, :] = jnp.broadcast_to(k_sq, (1, LANES))
    pad_row = lax.broadcasted_iota(jnp.int32, (VT_ROWS - V_HEAD_DIM, vt.shape[1]), 0)
    ones_pad = jnp.where(pad_row == 0, 1.0, 0.0).astype(BF16)
    for hd in range(N_HEADS):
        lo = hd * VT_ROWS
        v_ref[lo:lo + V_HEAD_DIM, :] = vt[hd * V_HEAD_DIM:(hd + 1) * V_HEAD_DIM, :].astype(BF16)
        v_ref[lo + V_HEAD_DIM:lo + VT_ROWS, :] = ones_pad


def _pre_call(x, pos, g_pre, w_in_p, g_q, w_uqt, g_kv, w_uk_p, w_uvt, inv_rows):
    s = x.shape[0]
    tm = PRE_TM
    row = lambda w: pl.BlockSpec((tm, w), lambda i: (i, 0))
    full = lambda a: pl.BlockSpec(a.shape, lambda i: (0,) * a.ndim)
    heads = lambda n: pl.BlockSpec((n, tm, LANES), lambda i: (0, i, 0))
    return pl.pallas_call(
        _pre_kernel,
        grid=(s // tm,),
        in_specs=[row(D_MODEL), pl.BlockSpec((1, tm), lambda i: (0, i)), full(g_pre),
                  full(w_in_p), full(g_q), full(w_uqt), full(g_kv), full(w_uk_p), full(w_uvt),
                  full(inv_rows)],
        out_specs=[pl.BlockSpec((N_HEADS, LANES, tm), lambda i: (0, 0, i)), heads(N_HEADS),
                   pl.BlockSpec((N_HEADS * VT_ROWS, tm), lambda i: (0, i)),
                   row(POOL_WIDTH), row(D_MODEL), row(D_MODEL),
                   pl.BlockSpec((N_HEADS, tm), lambda i: (0, i)),
                   pl.BlockSpec((1, N_HEADS, LANES), lambda i: (i, 0, 0))],
        out_shape=[jax.ShapeDtypeStruct((N_HEADS, LANES, s), BF16),
                   jax.ShapeDtypeStruct((N_HEADS, s, LANES), BF16),
                   jax.ShapeDtypeStruct((N_HEADS * VT_ROWS, s), BF16),
                   jax.ShapeDtypeStruct((s, POOL_WIDTH), F32),
                   jax.ShapeDtypeStruct((s, D_MODEL), F32),
                   jax.ShapeDtypeStruct((s, D_MODEL), F32),
                   jax.ShapeDtypeStruct((N_HEADS, s), F32),
                   jax.ShapeDtypeStruct((s // tm, N_HEADS, LANES), F32)],
        compiler_params=pltpu.CompilerParams(
            dimension_semantics=("parallel",), vmem_limit_bytes=VMEM_LIMIT),
        name="pre_proj",
    )(x, pos, g_pre, w_in_p, g_q, w_uqt, g_kv, w_uk_p, w_uvt, inv_rows)


def _attn_kernel(qt_ref, k_ref, vt_ref, qn_ref, kn_ref, kn_next_ref, o_ref, m_ref, acc_ref, flag_ref):
    j = pl.program_id(1)

    def head_rows(hd):
        return slice(hd * VT_ROWS, (hd + 1) * VT_ROWS)

    def within_headroom(kn_blk_ref):
        k_norm = jnp.sqrt(jnp.max(kn_blk_ref[...], axis=0)[:, :1])
        bound = jnp.sqrt(qn_ref[...]) * k_norm
        excess = jnp.max(bound - m_ref[...].reshape(bound.shape))
        return (excess <= EXP_HEADROOM).astype(jnp.int32)

    @pl.when(j == 0)
    def _init():
        for hd in range(N_HEADS):
            m_ref[hd] = _dot(k_ref[hd, 0:BF16_ROWS, :], qt_ref[hd])[0:1, :]
        acc_ref[...] = jnp.zeros(acc_ref.shape, F32)
        flag_ref[0] = within_headroom(kn_ref)

    streaming = flag_ref[0] == 1

    @pl.when(streaming)
    def _stream():
        flag_ref[0] = within_headroom(kn_next_ref)
        scores = {}
        for t in range(N_HEADS + QK_LEAD):
            if t < N_HEADS:
                scores[t] = _dot(k_ref[t], qt_ref[t])
            hd = t - QK_LEAD
            if hd >= 0:
                p = jnp.exp2(scores.pop(hd) - m_ref[hd]).astype(BF16)
                rows = head_rows(hd)
                acc_ref[rows, :] = acc_ref[rows, :] + _dot(vt_ref[rows, :], p)

    @pl.when(jnp.logical_not(streaming))
    def _two_pass():
        scores, probs, alphas = {}, {}, {}
        for t in range(N_HEADS + QK_LEAD + PV_LAG):
            if t < N_HEADS:
                scores[t] = _dot(k_ref[t], qt_ref[t])
            hd = t - QK_LEAD
            if 0 <= hd < N_HEADS:
                s = scores.pop(hd)
                m_prev = m_ref[hd]
                m_new = jnp.maximum(m_prev, jnp.max(s, axis=0, keepdims=True))
                alphas[hd] = jnp.exp2(m_prev - m_new)
                probs[hd] = jnp.exp2(s - m_new).astype(BF16)
                m_ref[hd] = m_new
            hd = t - QK_LEAD - PV_LAG
            if 0 <= hd < N_HEADS:
                rows = head_rows(hd)
                pv = _dot(vt_ref[rows, :], probs.pop(hd))
                acc_ref[rows, :] = alphas.pop(hd) * acc_ref[rows, :] + pv
        flag_ref[0] = within_headroom(kn_next_ref)

    @pl.when(j == pl.num_programs(1) - 1)
    def _finish():
        outs = []
        for hd in range(N_HEADS):
            lo = hd * VT_ROWS
            denom = acc_ref[lo + V_HEAD_DIM:lo + V_HEAD_DIM + 1, :]
            outs.append(acc_ref[lo:lo + V_HEAD_DIM, :] / denom)
        o_ref[...] = jnp.concatenate(outs, axis=0).T.astype(o_ref.dtype)


def _attn_call(qt, k, vt, qn, kn):
    s = k.shape[1]
    tq, tk = ATTN_TQ, ATTN_TK
    width = N_HEADS * V_HEAD_DIM
    kn_rows = tk // PRE_TM
    last_kv = s // tk - 1
    return pl.pallas_call(
        _attn_kernel,
        grid=(s // tq, s // tk),
        in_specs=[pl.BlockSpec((N_HEADS, LANES, tq), lambda i, j: (0, 0, i)),
                  pl.BlockSpec((N_HEADS, tk, LANES), lambda i, j: (0, j, 0)),
                  pl.BlockSpec((N_HEADS * VT_ROWS, tk), lambda i, j: (0, j)),
                  pl.BlockSpec((N_HEADS, tq), lambda i, j: (0, i)),
                  pl.BlockSpec((kn_rows, N_HEADS, LANES), lambda i, j: (j, 0, 0)),
                  pl.BlockSpec((kn_rows, N_HEADS, LANES),
                               lambda i, j: (jnp.minimum(j + 1, last_kv), 0, 0))],
        out_specs=pl.BlockSpec((tq, width), lambda i, j: (i, 0)),
        out_shape=jax.ShapeDtypeStruct((s, width), BF16),
        scratch_shapes=[pltpu.VMEM((N_HEADS, 1, tq), F32),
                        pltpu.VMEM((N_HEADS * VT_ROWS, tq), F32),
                        pltpu.SMEM((1,), jnp.int32)],
        compiler_params=pltpu.CompilerParams(
            dimension_semantics=("parallel", "arbitrary"), vmem_limit_bytes=VMEM_LIMIT),
        name="mla_attention",
    )(qt, k, vt, qn, kn, kn)


def _mix_kernel(x_ref, attn_ref, u_ref, up_ref, un_ref, ga_ref, gb_ref,
                w_oa_ref, w_pg_ref, pscale_ref, w_ob_ref, w_out_ref, g_post_ref,
                o_ref, ext_ref, *, seq_len):
    i = pl.program_id(0)
    tm = u_ref.shape[0]
    ext_ref[0:HALO, :] = jnp.where(i > 0, up_ref[...], 0.0)
    ext_ref[HALO:HALO + tm, :] = u_ref[...]
    ext_ref[HALO + tm:, :] = jnp.where(i < pl.num_programs(0) - 1, un_ref[...], 0.0)

    t = i * tm + lax.broadcasted_iota(jnp.int32, (tm, 1), 0)
    mixed = []
    for g, w in enumerate(POOL_WINDOWS):
        left = w // 2
        right = w - left - 1
        cols = slice(g * POOL_GROUP_DIM, (g + 1) * POOL_GROUP_DIM)
        win = ext_ref[HALO - left:HALO - left + tm, cols]
        for d in range(-left + 1, right + 1):
            win = win + ext_ref[HALO + d:HALO + d + tm, cols]
        cnt = (jnp.minimum(t + right, seq_len - 1) - jnp.maximum(t - left, 0) + 1).astype(F32)
        pooled = win / cnt - u_ref[:, cols]
        mixed.append(_dot(pooled.astype(BF16), w_pg_ref[g]))
    pool = jnp.concatenate(mixed, axis=1) * pscale_ref[...]

    a = _dot(attn_ref[...], w_oa_ref[...])
    b = _dot(pool.astype(BF16), w_ob_ref[...])
    merged = ga_ref[...] * a + gb_ref[...] * b
    y = _dot(merged.astype(BF16), w_out_ref[...])
    o_ref[...] = x_ref[...] + _rms(y, g_post_ref[...])


def _mix_call(x, attn, u, ga, gb, w_oa, w_pg, pscale, w_ob, w_out, g_post):
    s = x.shape[0]
    tm = MIX_TM
    nb = tm // HALO
    last = s // HALO - 1
    row = lambda w: pl.BlockSpec((tm, w), lambda i: (i, 0))
    full = lambda a: pl.BlockSpec(a.shape, lambda i: (0,) * a.ndim)
    prev = pl.BlockSpec((HALO, POOL_WIDTH), lambda i: (jnp.maximum(i * nb - 1, 0), 0))
    nxt = pl.BlockSpec((HALO, POOL_WIDTH), lambda i: (jnp.minimum((i + 1) * nb, last), 0))
    return pl.pallas_call(
        functools.partial(_mix_kernel, seq_len=s),
        grid=(s // tm,),
        in_specs=[row(D_MODEL), row(POOL_WIDTH), row(POOL_WIDTH), prev, nxt,
                  row(D_MODEL), row(D_MODEL),
                  full(w_oa), full(w_pg), full(pscale), full(w_ob), full(w_out), full(g_post)],
        out_specs=row(D_MODEL),
        out_shape=jax.ShapeDtypeStruct((s, D_MODEL), F32),
        scratch_shapes=[pltpu.VMEM((tm + 2 * HALO, POOL_WIDTH), F32)],
        compiler_params=pltpu.CompilerParams(
            dimension_semantics=("parallel",), vmem_limit_bytes=VMEM_LIMIT),
        name="mix_merge",
    )(x, attn, u, u, u, ga, gb, w_oa, w_pg, pscale, w_ob, w_out, g_post)


def _ffn_kernel(x_ref, g_pre_ref, w_gu_ref, w_dn_ref, g_post_ref, o_ref):
    x = x_ref[...]
    hf = _rms(x, g_pre_ref[...]).astype(BF16)
    gu = _dot(hf, w_gu_ref[...])
    act = jax.nn.silu(gu[:, :D_FF]) * gu[:, D_FF:]
    ff = _dot(act.astype(BF16), w_dn_ref[...])
    o_ref[...] = x + _rms(ff, g_post_ref[...])


def _ffn_call(x, g_pre, w_gu, w_dn, g_post):
    s = x.shape[0]
    tm = FFN_TM
    row = pl.BlockSpec((tm, D_MODEL), lambda i: (i, 0))
    full = lambda a: pl.BlockSpec(a.shape, lambda i: (0,) * a.ndim, pipeline_mode=pl.Buffered(1))
    return pl.pallas_call(
        _ffn_kernel,
        grid=(s // tm,),
        in_specs=[row, full(g_pre), full(w_gu), full(w_dn), full(g_post)],
        out_specs=row,
        out_shape=jax.ShapeDtypeStruct((s, D_MODEL), F32),
        compiler_params=pltpu.CompilerParams(
            dimension_semantics=("parallel",), vmem_limit_bytes=VMEM_LIMIT),
        name="swiglu_ffn",
    )(x, g_pre, w_gu, w_dn, g_post)


def _head_tiles(w, per_head, keep):
    r = w.shape[0]
    w = w.reshape(r, N_HEADS, per_head)[:, :, :keep]
    return jnp.pad(w, ((0, 0), (0, 0), (0, LANES - keep))).reshape(r, HEAD_W)


def kernel(x, positions, g_mix_pre, w_in, g_q_lat, w_uq, g_kv_lat, w_ukv, w_o_attn,
           w_pool_group, pool_scale, w_o_pool, w_out, g_mix_post, g_ffn_pre,
           w_gate_up, w_down, g_ffn_post):
    b, s, d = x.shape
    assert b == 1 and d == D_MODEL
    assert s % max(PRE_TM, ATTN_TQ, ATTN_TK, MIX_TM, FFN_TM) == 0
    vec = lambda g: g.reshape(1, -1).astype(F32)

    o_kr = Q_LORA_RANK + KV_LORA_RANK
    zeros = lambda n: jnp.zeros((D_MODEL, n), w_in.dtype)
    w_in_p = jnp.concatenate(
        [w_in[:, :o_kr], zeros(QK_NOPE_DIM), w_in[:, o_kr:o_kr + QK_ROPE_DIM],
         zeros(LANES - QK_HEAD_DIM), w_in[:, o_kr + QK_ROPE_DIM:]], axis=1).astype(BF16)
    w_uqt = _head_tiles(w_uq, QK_HEAD_DIM, QK_HEAD_DIM).T.astype(BF16)
    w_uk_p = _head_tiles(w_ukv, QK_NOPE_DIM + V_HEAD_DIM, QK_NOPE_DIM).astype(BF16)
    w_uv = w_ukv.reshape(KV_LORA_RANK, N_HEADS, QK_NOPE_DIM + V_HEAD_DIM)[:, :, QK_NOPE_DIM:]
    w_uvt = w_uv.reshape(KV_LORA_RANK, N_HEADS * V_HEAD_DIM).T.astype(BF16)

    inv = 1.0 / (ROPE_BASE ** (jnp.arange(0, QK_ROPE_DIM, 2, dtype=F32) / QK_ROPE_DIM))
    inv_rows = inv.reshape(QK_ROPE_DIM // 2, 1)

    x2 = x.reshape(s, d)
    q, k, v, u, ga, gb, qn, kn = _pre_call(
        x2, positions.reshape(1, s), vec(g_mix_pre), w_in_p, vec(g_q_lat), w_uqt,
        vec(g_kv_lat), w_uk_p, w_uvt, inv_rows)
    attn = _attn_call(q, k, v, qn, kn)
    x1 = _mix_call(x2, attn, u, ga, gb, w_o_attn.astype(BF16), w_pool_group.astype(BF16),
                   vec(pool_scale), w_o_pool.astype(BF16), w_out.astype(BF16), vec(g_mix_post))
    out = _ffn_call(x1, vec(g_ffn_pre), w_gate_up.astype(BF16), w_down.astype(BF16),
                    vec(g_ffn_post))
    return out.reshape(b, s, d)
```

```python
import functools
import math

import jax
import jax.numpy as jnp
from jax import lax
from jax.experimental import pallas as pl
from jax.experimental.pallas import tpu as pltpu

D_MODEL = 1024
N_HEADS = 8
QK_NOPE_DIM = 64
QK_ROPE_DIM = 32
V_HEAD_DIM = 64
Q_LORA_RANK = 384
KV_LORA_RANK = 256
QK_HEAD_DIM = QK_NOPE_DIM + QK_ROPE_DIM
ROPE_BASE = 10000.0
POOL_WINDOWS = (2, 4, 8, 16)
POOL_GROUP_DIM = 128
POOL_WIDTH = len(POOL_WINDOWS) * POOL_GROUP_DIM
D_FF = 2816
NORM_EPS = 1e-6

LANES = 128
SUBLANES = 8
HALO = 8
HEAD_W = N_HEADS * LANES
VMEM_LIMIT = 56 * 1024 * 1024

Q_SCALE = (QK_HEAD_DIM ** -0.5) * math.log2(math.e)

PRE_TM = 512
ATTN_TQ = 1024
ATTN_TK = 2048
QK_LEAD = 1
PV_LAG = 1
EXP_HEADROOM = 60.0
BF16_ROWS = 16
VT_ROWS = V_HEAD_DIM + BF16_ROWS
MIX_TM = 512
FFN_TM = 512

F32 = jnp.float32
BF16 = jnp.bfloat16


def _rms(x, g):
    return x * lax.rsqrt(jnp.mean(x * x, axis=-1, keepdims=True) + NORM_EPS) * g


def _dot(a, b):
    return jnp.dot(a, b, preferred_element_type=F32)


def _pre_kernel(x_ref, pos_ref, g_pre_ref, w_in_ref, g_q_ref, w_uqt_ref, g_kv_ref,
                w_uk_ref, w_uvt_ref, inv_ref,
                q_ref, k_ref, v_ref, u_ref, ga_ref, gb_ref, qn_ref, kn_ref):
    h = _rms(x_ref[...], g_pre_ref[...]).astype(BF16)
    p = _dot(h, w_in_ref[...])
    o = 0
    c_q = p[:, o:o + Q_LORA_RANK]; o += Q_LORA_RANK
    c_kv = p[:, o:o + KV_LORA_RANK]; o += KV_LORA_RANK
    kr = p[:, o:o + LANES]; o += LANES
    u_ref[...] = p[:, o:o + POOL_WIDTH]; o += POOL_WIDTH
    ga_ref[...] = jax.nn.sigmoid(p[:, o:o + D_MODEL]); o += D_MODEL
    gb_ref[...] = jax.nn.sigmoid(p[:, o:o + D_MODEL])

    half = QK_ROPE_DIM // 2
    ang = inv_ref[...] * pos_ref[...].astype(F32)
    cos = jnp.cos(ang)
    sin = jnp.sin(ang)

    def rope_t(t):
        t1 = t[QK_NOPE_DIM:QK_NOPE_DIM + half]
        t2 = t[QK_NOPE_DIM + half:QK_HEAD_DIM]
        return jnp.concatenate(
            [t[:QK_NOPE_DIM], t1 * cos - t2 * sin, t2 * cos + t1 * sin, t[QK_HEAD_DIM:]], axis=0)

    cq_t = _rms(c_q, g_q_ref[...]).T.astype(BF16)
    ckv_n = _rms(c_kv, g_kv_ref[...])
    q_t = _dot(w_uqt_ref[...], cq_t)
    vt = _dot(w_uvt_ref[...], ckv_n.T.astype(BF16))
    k = _dot(ckv_n.astype(BF16), w_uk_ref[...])
    kr = rope_t(kr.T).T
    for hd in range(N_HEADS):
        sl = slice(hd * LANES, (hd + 1) * LANES)
        qt = (rope_t(q_t[sl, :]) * Q_SCALE).astype(BF16)
        kb = (k[:, sl] + kr).astype(BF16)
        q_ref[hd] = qt
        k_ref[hd] = kb
        qf = qt.astype(F32)
        kf = kb.astype(F32)
        qn_ref[hd:hd + 1, :] = jnp.sum(qf * qf, axis=0, keepdims=True)
        k_sq = jnp.max(jnp.sum(kf * kf, axis=1, keepdims=True), axis=0, keepdims=True)
        kn_ref[0, hd:hd + 1, :] = jnp.broadcast_to(k_sq, (1, LANES))
    pad_row = lax.broadcasted_iota(jnp.int32, (VT_ROWS - V_HEAD_DIM, vt.shape[1]), 0)
    ones_pad = jnp.where(pad_row == 0, 1.0, 0.0).astype(BF16)
    for hd in range(N_HEADS):
        lo = hd * VT_ROWS
        v_ref[lo:lo + V_HEAD_DIM, :] = vt[hd * V_HEAD_DIM:(hd + 1) * V_HEAD_DIM, :].astype(BF16)
        v_ref[lo + V_HEAD_DIM:lo + VT_ROWS, :] = ones_pad


def _pre_call(x, pos, g_pre, w_in_p, g_q, w_uqt, g_kv, w_uk_p, w_uvt, inv_rows):
    s = x.shape[0]
    tm = PRE_TM
    row = lambda w: pl.BlockSpec((tm, w), lambda i: (i, 0))
    full = lambda a: pl.BlockSpec(a.shape, lambda i: (0,) * a.ndim)
    heads = lambda n: pl.BlockSpec((n, tm, LANES), lambda i: (0, i, 0))
    return pl.pallas_call(
        _pre_kernel,
        grid=(s // tm,),
        in_specs=[row(D_MODEL), pl.BlockSpec((1, tm), lambda i: (0, i)), full(g_pre),
                  full(w_in_p), full(g_q), full(w_uqt), full(g_kv), full(w_uk_p), full(w_uvt),
                  full(inv_rows)],
        out_specs=[pl.BlockSpec((N_HEADS, LANES, tm), lambda i: (0, 0, i)), heads(N_HEADS),
                   pl.BlockSpec((N_HEADS * VT_ROWS, tm), lambda i: (0, i)),
                   row(POOL_WIDTH), row(D_MODEL), row(D_MODEL),
                   pl.BlockSpec((N_HEADS, tm), lambda i: (0, i)),
                   pl.BlockSpec((1, N_HEADS, LANES), lambda i: (i, 0, 0))],
        out_shape=[jax.ShapeDtypeStruct((N_HEADS, LANES, s), BF16),
                   jax.ShapeDtypeStruct((N_HEADS, s, LANES), BF16),
                   jax.ShapeDtypeStruct((N_HEADS * VT_ROWS, s), BF16),
                   jax.ShapeDtypeStruct((s, POOL_WIDTH), F32),
                   jax.ShapeDtypeStruct((s, D_MODEL), F32),
                   jax.ShapeDtypeStruct((s, D_MODEL), F32),
                   jax.ShapeDtypeStruct((N_HEADS, s), F32),
                   jax.ShapeDtypeStruct((s // tm, N_HEADS, LANES), F32)],
        compiler_params=pltpu.CompilerParams(
            dimension_semantics=("parallel",), vmem_limit_bytes=VMEM_LIMIT),
        name="pre_proj",
    )(x, pos, g_pre, w_in_p, g_q, w_uqt, g_kv, w_uk_p, w_uvt, inv_rows)


def _attn_kernel(qt_ref, k_ref, vt_ref, qn_ref, kn_ref, kn_next_ref, o_ref, m_ref, acc_ref, flag_ref):
    j = pl.program_id(1)

    def head_rows(hd):
        return slice(hd * VT_ROWS, (hd + 1) * VT_ROWS)

    def within_headroom(kn_blk_ref):
        k_norm = jnp.sqrt(jnp.max(kn_blk_ref[...], axis=0)[:, :1])
        bound = jnp.sqrt(qn_ref[...]) * k_norm
        excess = jnp.max(bound - m_ref[...].reshape(bound.shape))
        return (excess <= EXP_HEADROOM).astype(jnp.int32)

    @pl.when(j == 0)
    def _init():
        for hd in range(N_HEADS):
            m_ref[hd] = _dot(k_ref[hd, 0:BF16_ROWS, :], qt_ref[hd])[0:1, :]
        acc_ref[...] = jnp.zeros(acc_ref.shape, F32)
        flag_ref[0] = within_headroom(kn_ref)

    streaming = flag_ref[0] == 1

    @pl.when(streaming)
    def _stream():
        flag_ref[0] = within_headroom(kn_next_ref)
        scores = {}
        for t in range(N_HEADS + QK_LEAD):
            if t < N_HEADS:
                scores[t] = _dot(k_ref[t], qt_ref[t])
            hd = t - QK_LEAD
            if hd >= 0:
                p = jnp.exp2(scores.pop(hd) - m_ref[hd]).astype(BF16)
                rows = head_rows(hd)
                acc_ref[rows, :] = acc_ref[rows, :] + _dot(vt_ref[rows, :], p)

    @pl.when(jnp.logical_not(streaming))
    def _two_pass():
        scores, probs, alphas = {}, {}, {}
        for t in range(N_HEADS + QK_LEAD + PV_LAG):
            if t < N_HEADS:
                scores[t] = _dot(k_ref[t], qt_ref[t])
            hd = t - QK_LEAD
            if 0 <= hd < N_HEADS:
                s = scores.pop(hd)
                m_prev = m_ref[hd]
                m_new = jnp.maximum(m_prev, jnp.max(s, axis=0, keepdims=True))
                alphas[hd] = jnp.exp2(m_prev - m_new)
                probs[hd] = jnp.exp2(s - m_new).astype(BF16)
                m_ref[hd] = m_new
            hd = t - QK_LEAD - PV_LAG
            if 0 <= hd < N_HEADS:
                rows = head_rows(hd)
                pv = _dot(vt_ref[rows, :], probs.pop(hd))
                acc_ref[rows, :] = alphas.pop(hd) * acc_ref[rows, :] + pv
        flag_ref[0] = within_headroom(kn_next_ref)

    @pl.when(j == pl.num_programs(1) - 1)
    def _finish():
        outs = []
        for hd in range(N_HEADS):
            lo = hd * VT_ROWS
            denom = acc_ref[lo + V_HEAD_DIM:lo + V_HEAD_DIM + 1, :]
            outs.append(acc_ref[lo:lo + V_HEAD_DIM, :] / denom)
        o_ref[...] = jnp.concatenate(outs, axis=0).T.astype(o_ref.dtype)


def _attn_call(qt, k, vt, qn, kn):
    s = k.shape[1]
    tq, tk = ATTN_TQ, ATTN_TK
    width = N_HEADS * V_HEAD_DIM
    kn_rows = tk // PRE_TM
    last_kv = s // tk - 1
    return pl.pallas_call(
        _attn_kernel,
        grid=(s // tq, s // tk),
        in_specs=[pl.BlockSpec((N_HEADS, LANES, tq), lambda i, j: (0, 0, i)),
                  pl.BlockSpec((N_HEADS, tk, LANES), lambda i, j: (0, j, 0)),
                  pl.BlockSpec((N_HEADS * VT_ROWS, tk), lambda i, j: (0, j)),
                  pl.BlockSpec((N_HEADS, tq), lambda i, j: (0, i)),
                  pl.BlockSpec((kn_rows, N_HEADS, LANES), lambda i, j: (j, 0, 0)),
                  pl.BlockSpec((kn_rows, N_HEADS, LANES),
                               lambda i, j: (jnp.minimum(j + 1, last_kv), 0, 0))],
        out_specs=pl.BlockSpec((tq, width), lambda i, j: (i, 0)),
        out_shape=jax.ShapeDtypeStruct((s, width), BF16),
        scratch_shapes=[pltpu.VMEM((N_HEADS, 1, tq), F32),
                        pltpu.VMEM((N_HEADS * VT_ROWS, tq), F32),
                        pltpu.SMEM((1,), jnp.int32)],
        compiler_params=pltpu.CompilerParams(
            dimension_semantics=("parallel", "arbitrary"), vmem_limit_bytes=VMEM_LIMIT),
        name="mla_attention",
    )(qt, k, vt, qn, kn, kn)


def _mix_kernel(x_ref, attn_ref, u_ref, up_ref, un_ref, ga_ref, gb_ref,
                w_oa_ref, w_pg_ref, pscale_ref, w_ob_ref, w_out_ref, g_post_ref,
                o_ref, ext_ref, *, seq_len):
    i = pl.program_id(0)
    tm = u_ref.shape[0]
    ext_ref[0:HALO, :] = jnp.where(i > 0, up_ref[...], 0.0)
    ext_ref[HALO:HALO + tm, :] = u_ref[...]
    ext_ref[HALO + tm:, :] = jnp.where(i < pl.num_programs(0) - 1, un_ref[...], 0.0)

    t = i * tm + lax.broadcasted_iota(jnp.int32, (tm, 1), 0)
    mixed = []
    for g, w in enumerate(POOL_WINDOWS):
        left = w // 2
        right = w - left - 1
        cols = slice(g * POOL_GROUP_DIM, (g + 1) * POOL_GROUP_DIM)
        win = ext_ref[HALO - left:HALO - left + tm, cols]
        for d in range(-left + 1, right + 1):
            win = win + ext_ref[HALO + d:HALO + d + tm, cols]
        cnt = (jnp.minimum(t + right, seq_len - 1) - jnp.maximum(t - left, 0) + 1).astype(F32)
        pooled = win / cnt - u_ref[:, cols]
        mixed.append(_dot(pooled.astype(BF16), w_pg_ref[g]))
    pool = jnp.concatenate(mixed, axis=1) * pscale_ref[...]

    a = _dot(attn_ref[...], w_oa_ref[...])
    b = _dot(pool.astype(BF16), w_ob_ref[...])
    merged = ga_ref[...] * a + gb_ref[...] * b
    y = _dot(merged.astype(BF16), w_out_ref[...])
    o_ref[...] = x_ref[...] + _rms(y, g_post_ref[...])


def _mix_call(x, attn, u, ga, gb, w_oa, w_pg, pscale, w_ob, w_out, g_post):
    s = x.shape[0]
    tm = MIX_TM
    nb = tm // HALO
    last = s // HALO - 1
    row = lambda w: pl.BlockSpec((tm, w), lambda i: (i, 0))
    full = lambda a: pl.BlockSpec(a.shape, lambda i: (0,) * a.ndim)
    prev = pl.BlockSpec((HALO, POOL_WIDTH), lambda i: (jnp.maximum(i * nb - 1, 0), 0))
    nxt = pl.BlockSpec((HALO, POOL_WIDTH), lambda i: (jnp.minimum((i + 1) * nb, last), 0))
    return pl.pallas_call(
        functools.partial(_mix_kernel, seq_len=s),
        grid=(s // tm,),
        in_specs=[row(D_MODEL), row(POOL_WIDTH), row(POOL_WIDTH), prev, nxt,
                  row(D_MODEL), row(D_MODEL),
                  full(w_oa), full(w_pg), full(pscale), full(w_ob), full(w_out), full(g_post)],
        out_specs=row(D_MODEL),
        out_shape=jax.ShapeDtypeStruct((s, D_MODEL), F32),
        scratch_shapes=[pltpu.VMEM((tm + 2 * HALO, POOL_WIDTH), F32)],
        compiler_params=pltpu.CompilerParams(
            dimension_semantics=("parallel",), vmem_limit_bytes=VMEM_LIMIT),
        name="mix_merge",
    )(x, attn, u, u, u, ga, gb, w_oa, w_pg, pscale, w_ob, w_out, g_post)


def _ffn_kernel(x_ref, g_pre_ref, w_gu_ref, w_dn_ref, g_post_ref, o_ref):
    x = x_ref[...]
    hf = _rms(x, g_pre_ref[...]).astype(BF16)
    gu = _dot(hf, w_gu_ref[...])
    act = jax.nn.silu(gu[:, :D_FF]) * gu[:, D_FF:]
    ff = _dot(act.astype(BF16), w_dn_ref[...])
    o_ref[...] = x + _rms(ff, g_post_ref[...])


def _ffn_call(x, g_pre, w_gu, w_dn, g_post):
    s = x.shape[0]
    tm = FFN_TM
    row = pl.BlockSpec((tm, D_MODEL), lambda i: (i, 0))
    full = lambda a: pl.BlockSpec(a.shape, lambda i: (0,) * a.ndim, pipeline_mode=pl.Buffered(1))
    return pl.pallas_call(
        _ffn_kernel,
        grid=(s // tm,),
        in_specs=[row, full(g_pre), full(w_gu), full(w_dn), full(g_post)],
        out_specs=row,
        out_shape=jax.ShapeDtypeStruct((s, D_MODEL), F32),
        compiler_params=pltpu.CompilerParams(
            dimension_semantics=("parallel",), vmem_limit_bytes=VMEM_LIMIT),
        name="swiglu_ffn",
    )(x, g_pre, w_gu, w_dn, g_post)


def _head_tiles(w, per_head, keep):
    r = w.shape[0]
    w = w.reshape(r, N_HEADS, per_head)[:, :, :keep]
    return jnp.pad(w, ((0, 0), (0, 0), (0, LANES - keep))).reshape(r, HEAD_W)


def kernel(x, positions, g_mix_pre, w_in, g_q_lat, w_uq, g_kv_lat, w_ukv, w_o_attn,
           w_pool_group, pool_scale, w_o_pool, w_out, g_mix_post, g_ffn_pre,
           w_gate_up, w_down, g_ffn_post):
    b, s, d = x.shape
    assert b == 1 and d == D_MODEL
    assert s % max(PRE_TM, ATTN_TQ, ATTN_TK, MIX_TM, FFN_TM) == 0
    vec = lambda g: g.reshape(1, -1).astype(F32)

    o_kr = Q_LORA_RANK + KV_LORA_RANK
    zeros = lambda n: jnp.zeros((D_MODEL, n), BF16)
    w_in_b = w_in.astype(BF16)
    w_in_p = jnp.concatenate(
        [w_in_b[:, :o_kr], zeros(QK_NOPE_DIM), w_in_b[:, o_kr:o_kr + QK_ROPE_DIM],
         zeros(LANES - QK_HEAD_DIM), w_in_b[:, o_kr + QK_ROPE_DIM:]], axis=1)
    w_uqt = _head_tiles(w_uq, QK_HEAD_DIM, QK_HEAD_DIM).T.astype(BF16)
    w_uk_p = _head_tiles(w_ukv, QK_NOPE_DIM + V_HEAD_DIM, QK_NOPE_DIM).astype(BF16)
    w_uv = w_ukv.reshape(KV_LORA_RANK, N_HEADS, QK_NOPE_DIM + V_HEAD_DIM)[:, :, QK_NOPE_DIM:]
    w_uvt = w_uv.reshape(KV_LORA_RANK, N_HEADS * V_HEAD_DIM).T.astype(BF16)

    inv = 1.0 / (ROPE_BASE ** (jnp.arange(0, QK_ROPE_DIM, 2, dtype=F32) / QK_ROPE_DIM))
    inv_rows = inv.reshape(QK_ROPE_DIM // 2, 1)

    x2 = x.reshape(s, d)
    q, k, v, u, ga, gb, qn, kn = _pre_call(
        x2, positions.reshape(1, s), vec(g_mix_pre), w_in_p, vec(g_q_lat), w_uqt,
        vec(g_kv_lat), w_uk_p, w_uvt, inv_rows)
    attn = _attn_call(q, k, v, qn, kn)
    x1 = _mix_call(x2, attn, u, ga, gb, w_o_attn.astype(BF16), w_pool_group.astype(BF16),
                   vec(pool_scale), w_o_pool.astype(BF16), w_out.astype(BF16), vec(g_mix_post))
    out = _ffn_call(x1, vec(g_ffn_pre), w_gate_up.astype(BF16), w_down.astype(BF16),
                    vec(g_ffn_post))
    return out.reshape(b, s, d)
```

```python
import functools
import math

import jax
import jax.numpy as jnp
from jax import lax
from jax.experimental import pallas as pl
from jax.experimental.pallas import tpu as pltpu

D_MODEL = 1024
N_HEADS = 8
QK_NOPE_DIM = 64
QK_ROPE_DIM = 32
V_HEAD_DIM = 64
Q_LORA_RANK = 384
KV_LORA_RANK = 256
QK_HEAD_DIM = QK_NOPE_DIM + QK_ROPE_DIM
ROPE_BASE = 10000.0
POOL_WINDOWS = (2, 4, 8, 16)
POOL_GROUP_DIM = 128
POOL_WIDTH = len(POOL_WINDOWS) * POOL_GROUP_DIM
D_FF = 2816
NORM_EPS = 1e-6

LANES = 128
HALO = 8
HEAD_W = N_HEADS * LANES
VMEM_LIMIT = 56 * 1024 * 1024

Q_SCALE = (QK_HEAD_DIM ** -0.5) * math.log2(math.e)

PRE_TM = 512
ATTN_TQ = 1024
ATTN_TK = 2048
QK_LEAD = 1
PV_LAG = 1
MXU_DIM = 256
UNIT_LEAD = 6
EXP_HEADROOM = 60.0
BF16_ROWS = 16
VT_ROWS = V_HEAD_DIM + BF16_ROWS
MIX_TM = 512
FFN_TM = 512

F32 = jnp.float32
BF16 = jnp.bfloat16


def _rms(x, g):
    return x * lax.rsqrt(jnp.mean(x * x, axis=-1, keepdims=True) + NORM_EPS) * g


def _dot(a, b):
    return jnp.dot(a, b, preferred_element_type=F32)


def _pre_kernel(x_ref, pos_ref, g_pre_ref, w_in_ref, g_q_ref, w_uqt_ref, g_kv_ref,
                w_uk_ref, w_uvt_ref, inv_ref,
                q_ref, k_ref, v_ref, u_ref, ga_ref, gb_ref, qn_ref, kn_ref):
    h = _rms(x_ref[...], g_pre_ref[...]).astype(BF16)
    p = _dot(h, w_in_ref[...])
    o = 0
    c_q = p[:, o:o + Q_LORA_RANK]; o += Q_LORA_RANK
    c_kv = p[:, o:o + KV_LORA_RANK]; o += KV_LORA_RANK
    kr = p[:, o:o + LANES]; o += LANES
    u_ref[...] = p[:, o:o + POOL_WIDTH]; o += POOL_WIDTH
    ga_ref[...] = jax.nn.sigmoid(p[:, o:o + D_MODEL]); o += D_MODEL
    gb_ref[...] = jax.nn.sigmoid(p[:, o:o + D_MODEL])

    half = QK_ROPE_DIM // 2
    ang = inv_ref[...] * pos_ref[...].astype(F32)
    cos = jnp.cos(ang)
    sin = jnp.sin(ang)

    def rope_t(t):
        t1 = t[QK_NOPE_DIM:QK_NOPE_DIM + half]
        t2 = t[QK_NOPE_DIM + half:QK_HEAD_DIM]
        return jnp.concatenate(
            [t[:QK_NOPE_DIM], t1 * cos - t2 * sin, t2 * cos + t1 * sin, t[QK_HEAD_DIM:]], axis=0)

    cq_t = _rms(c_q, g_q_ref[...]).T.astype(BF16)
    ckv_n = _rms(c_kv, g_kv_ref[...])
    q_t = _dot(w_uqt_ref[...], cq_t)
    vt = _dot(w_uvt_ref[...], ckv_n.T.astype(BF16))
    k = _dot(ckv_n.astype(BF16), w_uk_ref[...])
    kr = rope_t(kr.T).T
    for hd in range(N_HEADS):
        sl = slice(hd * LANES, (hd + 1) * LANES)
        qt = (rope_t(q_t[sl, :]) * Q_SCALE).astype(BF16)
        kb = (k[:, sl] + kr).astype(BF16)
        q_ref[hd] = qt
        k_ref[hd] = kb
        qf = qt.astype(F32)
        kf = kb.astype(F32)
        qn_ref[hd:hd + 1, :] = jnp.sum(qf * qf, axis=0, keepdims=True)
        k_sq = jnp.max(jnp.sum(kf * kf, axis=1, keepdims=True), axis=0, keepdims=True)
        kn_ref[0, hd:hd + 1, :] = jnp.broadcast_to(k_sq, (1, LANES))
    pad_row = lax.broadcasted_iota(jnp.int32, (VT_ROWS - V_HEAD_DIM, vt.shape[1]), 0)
    ones_pad = jnp.where(pad_row == 0, 1.0, 0.0).astype(BF16)
    for hd in range(N_HEADS):
        lo = hd * VT_ROWS
        v_ref[lo:lo + V_HEAD_DIM, :] = vt[hd * V_HEAD_DIM:(hd + 1) * V_HEAD_DIM, :].astype(BF16)
        v_ref[lo + V_HEAD_DIM:lo + VT_ROWS, :] = ones_pad


def _pre_call(x, pos, g_pre, w_in_p, g_q, w_uqt, g_kv, w_uk_p, w_uvt, inv_rows):
    s = x.shape[0]
    tm = PRE_TM
    row = lambda w: pl.BlockSpec((tm, w), lambda i: (i, 0))
    full = lambda a: pl.BlockSpec(a.shape, lambda i: (0,) * a.ndim)
    heads = lambda n: pl.BlockSpec((n, tm, LANES), lambda i: (0, i, 0))
    return pl.pallas_call(
        _pre_kernel,
        grid=(s // tm,),
        in_specs=[row(D_MODEL), pl.BlockSpec((1, tm), lambda i: (0, i)), full(g_pre),
                  full(w_in_p), full(g_q), full(w_uqt), full(g_kv), full(w_uk_p), full(w_uvt),
                  full(inv_rows)],
        out_specs=[pl.BlockSpec((N_HEADS, LANES, tm), lambda i: (0, 0, i)), heads(N_HEADS),
                   pl.BlockSpec((N_HEADS * VT_ROWS, tm), lambda i: (0, i)),
                   row(POOL_WIDTH), row(D_MODEL), row(D_MODEL),
                   pl.BlockSpec((N_HEADS, tm), lambda i: (0, i)),
                   pl.BlockSpec((1, N_HEADS, LANES), lambda i: (i, 0, 0))],
        out_shape=[jax.ShapeDtypeStruct((N_HEADS, LANES, s), BF16),
                   jax.ShapeDtypeStruct((N_HEADS, s, LANES), BF16),
                   jax.ShapeDtypeStruct((N_HEADS * VT_ROWS, s), BF16),
                   jax.ShapeDtypeStruct((s, POOL_WIDTH), F32),
                   jax.ShapeDtypeStruct((s, D_MODEL), F32),
                   jax.ShapeDtypeStruct((s, D_MODEL), F32),
                   jax.ShapeDtypeStruct((N_HEADS, s), F32),
                   jax.ShapeDtypeStruct((s // tm, N_HEADS, LANES), F32)],
        compiler_params=pltpu.CompilerParams(
            dimension_semantics=("parallel",), vmem_limit_bytes=VMEM_LIMIT),
        name="pre_proj",
    )(x, pos, g_pre, w_in_p, g_q, w_uqt, g_kv, w_uk_p, w_uvt, inv_rows)


def _attn_kernel(qt_ref, k_ref, vt_ref, qn_ref, kn_ref, kn_next_ref, o_ref, m_ref, acc_ref, flag_ref):
    j = pl.program_id(1)

    def head_rows(hd):
        return slice(hd * VT_ROWS, (hd + 1) * VT_ROWS)

    def within_headroom(kn_blk_ref):
        k_norm = jnp.sqrt(jnp.max(kn_blk_ref[...], axis=0)[:, :1])
        bound = jnp.sqrt(qn_ref[...]) * k_norm
        excess = jnp.max(bound - m_ref[...].reshape(bound.shape))
        return (excess <= EXP_HEADROOM).astype(jnp.int32)

    @pl.when(j == 0)
    def _init():
        for hd in range(N_HEADS):
            m_ref[hd] = _dot(k_ref[hd, 0:BF16_ROWS, :], qt_ref[hd])[0:1, :]
        acc_ref[...] = jnp.zeros(acc_ref.shape, F32)
        flag_ref[0] = within_headroom(kn_ref)

    streaming = flag_ref[0] == 1

    @pl.when(streaming)
    def _stream():
        flag_ref[0] = within_headroom(kn_next_ref)
        tk, tq = k_ref.shape[1], qt_ref.shape[2]
        units = [(hd, qc, kc) for hd in range(N_HEADS) for qc in range(0, tq, MXU_DIM)
                 for kc in range(0, tk, MXU_DIM)]
        scores, pvs = {}, {}
        for t in range(len(units) + UNIT_LEAD):
            if t < len(units):
                hd, qc, kc = units[t]
                scores[t] = _dot(k_ref[hd, kc:kc + MXU_DIM, :], qt_ref[hd, :, qc:qc + MXU_DIM])
            u = t - UNIT_LEAD
            if u >= 0:
                hd, qc, kc = units[u]
                rows = head_rows(hd)
                cols = slice(qc, qc + MXU_DIM)
                p = jnp.exp2(scores.pop(u) - m_ref[hd, :, cols]).astype(BF16)
                d = _dot(vt_ref[rows, kc:kc + MXU_DIM], p)
                pvs[hd, qc] = d if (hd, qc) not in pvs else pvs[hd, qc] + d
                if kc + MXU_DIM == tk:
                    acc_ref[rows, cols] = acc_ref[rows, cols] + pvs.pop((hd, qc))

    @pl.when(jnp.logical_not(streaming))
    def _two_pass():
        scores, probs, alphas = {}, {}, {}
        for t in range(N_HEADS + QK_LEAD + PV_LAG):
            if t < N_HEADS:
                scores[t] = _dot(k_ref[t], qt_ref[t])
            hd = t - QK_LEAD
            if 0 <= hd < N_HEADS:
                s = scores.pop(hd)
                m_prev = m_ref[hd]
                m_new = jnp.maximum(m_prev, jnp.max(s, axis=0, keepdims=True))
                alphas[hd] = jnp.exp2(m_prev - m_new)
                probs[hd] = jnp.exp2(s - m_new).astype(BF16)
                m_ref[hd] = m_new
            hd = t - QK_LEAD - PV_LAG
            if 0 <= hd < N_HEADS:
                rows = head_rows(hd)
                pv = _dot(vt_ref[rows, :], probs.pop(hd))
                acc_ref[rows, :] = alphas.pop(hd) * acc_ref[rows, :] + pv
        flag_ref[0] = within_headroom(kn_next_ref)

    @pl.when(j == pl.num_programs(1) - 1)
    def _finish():
        outs = []
        for hd in range(N_HEADS):
            lo = hd * VT_ROWS
            denom = acc_ref[lo + V_HEAD_DIM:lo + V_HEAD_DIM + 1, :]
            outs.append(acc_ref[lo:lo + V_HEAD_DIM, :] / denom)
        o_ref[...] = jnp.concatenate(outs, axis=0).T.astype(o_ref.dtype)


def _attn_call(qt, k, vt, qn, kn):
    s = k.shape[1]
    tq, tk = ATTN_TQ, ATTN_TK
    width = N_HEADS * V_HEAD_DIM
    kn_rows = tk // PRE_TM
    last_kv = s // tk - 1
    return pl.pallas_call(
        _attn_kernel,
        grid=(s // tq, s // tk),
        in_specs=[pl.BlockSpec((N_HEADS, LANES, tq), lambda i, j: (0, 0, i)),
                  pl.BlockSpec((N_HEADS, tk, LANES), lambda i, j: (0, j, 0)),
                  pl.BlockSpec((N_HEADS * VT_ROWS, tk), lambda i, j: (0, j)),
                  pl.BlockSpec((N_HEADS, tq), lambda i, j: (0, i)),
                  pl.BlockSpec((kn_rows, N_HEADS, LANES), lambda i, j: (j, 0, 0)),
                  pl.BlockSpec((kn_rows, N_HEADS, LANES),
                               lambda i, j: (jnp.minimum(j + 1, last_kv), 0, 0))],
        out_specs=pl.BlockSpec((tq, width), lambda i, j: (i, 0)),
        out_shape=jax.ShapeDtypeStruct((s, width), BF16),
        scratch_shapes=[pltpu.VMEM((N_HEADS, 1, tq), F32),
                        pltpu.VMEM((N_HEADS * VT_ROWS, tq), F32),
                        pltpu.SMEM((1,), jnp.int32)],
        compiler_params=pltpu.CompilerParams(
            dimension_semantics=("parallel", "arbitrary"), vmem_limit_bytes=VMEM_LIMIT),
        name="mla_attention",
    )(qt, k, vt, qn, kn, kn)


def _mix_kernel(x_ref, attn_ref, u_ref, up_ref, un_ref, ga_ref, gb_ref,
                w_oa_ref, w_pg_ref, pscale_ref, w_ob_ref, w_out_ref, g_post_ref,
                o_ref, ext_ref, *, seq_len):
    i = pl.program_id(0)
    tm = u_ref.shape[0]
    ext_ref[0:HALO, :] = jnp.where(i > 0, up_ref[...], 0.0)
    ext_ref[HALO:HALO + tm, :] = u_ref[...]
    ext_ref[HALO + tm:, :] = jnp.where(i < pl.num_programs(0) - 1, un_ref[...], 0.0)

    t = i * tm + lax.broadcasted_iota(jnp.int32, (tm, 1), 0)
    mixed = []
    for g, w in enumerate(POOL_WINDOWS):
        left = w // 2
        right = w - left - 1
        cols = slice(g * POOL_GROUP_DIM, (g + 1) * POOL_GROUP_DIM)
        win = ext_ref[HALO - left:HALO - left + tm, cols]
        for d in range(-left + 1, right + 1):
            win = win + ext_ref[HALO + d:HALO + d + tm, cols]
        cnt = (jnp.minimum(t + right, seq_len - 1) - jnp.maximum(t - left, 0) + 1).astype(F32)
        pooled = win / cnt - u_ref[:, cols]
        mixed.append(_dot(pooled.astype(BF16), w_pg_ref[g]))
    pool = jnp.concatenate(mixed, axis=1) * pscale_ref[...]

    a = _dot(attn_ref[...], w_oa_ref[...])
    b = _dot(pool.astype(BF16), w_ob_ref[...])
    merged = ga_ref[...] * a + gb_ref[...] * b
    y = _dot(merged.astype(BF16), w_out_ref[...])
    o_ref[...] = x_ref[...] + _rms(y, g_post_ref[...])


def _mix_call(x, attn, u, ga, gb, w_oa, w_pg, pscale, w_ob, w_out, g_post):
    s = x.shape[0]
    tm = MIX_TM
    nb = tm // HALO
    last = s // HALO - 1
    row = lambda w: pl.BlockSpec((tm, w), lambda i: (i, 0))
    full = lambda a: pl.BlockSpec(a.shape, lambda i: (0,) * a.ndim)
    prev = pl.BlockSpec((HALO, POOL_WIDTH), lambda i: (jnp.maximum(i * nb - 1, 0), 0))
    nxt = pl.BlockSpec((HALO, POOL_WIDTH), lambda i: (jnp.minimum((i + 1) * nb, last), 0))
    return pl.pallas_call(
        functools.partial(_mix_kernel, seq_len=s),
        grid=(s // tm,),
        in_specs=[row(D_MODEL), row(POOL_WIDTH), row(POOL_WIDTH), prev, nxt,
                  row(D_MODEL), row(D_MODEL),
                  full(w_oa), full(w_pg), full(pscale), full(w_ob), full(w_out), full(g_post)],
        out_specs=row(D_MODEL),
        out_shape=jax.ShapeDtypeStruct((s, D_MODEL), F32),
        scratch_shapes=[pltpu.VMEM((tm + 2 * HALO, POOL_WIDTH), F32)],
        compiler_params=pltpu.CompilerParams(
            dimension_semantics=("parallel",), vmem_limit_bytes=VMEM_LIMIT),
        name="mix_merge",
    )(x, attn, u, u, u, ga, gb, w_oa, w_pg, pscale, w_ob, w_out, g_post)


def _ffn_kernel(x_ref, g_pre_ref, w_gu_ref, w_dn_ref, g_post_ref, o_ref):
    x = x_ref[...]
    hf = _rms(x, g_pre_ref[...]).astype(BF16)
    gu = _dot(hf, w_gu_ref[...])
    act = jax.nn.silu(gu[:, :D_FF]) * gu[:, D_FF:]
    ff = _dot(act.astype(BF16), w_dn_ref[...])
    o_ref[...] = x + _rms(ff, g_post_ref[...])


def _ffn_call(x, g_pre, w_gu, w_dn, g_post):
    s = x.shape[0]
    tm = FFN_TM
    row = pl.BlockSpec((tm, D_MODEL), lambda i: (i, 0))
    full = lambda a: pl.BlockSpec(a.shape, lambda i: (0,) * a.ndim, pipeline_mode=pl.Buffered(1))
    return pl.pallas_call(
        _ffn_kernel,
        grid=(s // tm,),
        in_specs=[row, full(g_pre), full(w_gu), full(w_dn), full(g_post)],
        out_specs=row,
        out_shape=jax.ShapeDtypeStruct((s, D_MODEL), F32),
        compiler_params=pltpu.CompilerParams(
            dimension_semantics=("parallel",), vmem_limit_bytes=VMEM_LIMIT),
        name="swiglu_ffn",
    )(x, g_pre, w_gu, w_dn, g_post)


def _head_tiles(w, per_head, keep):
    r = w.shape[0]
    w = w.reshape(r, N_HEADS, per_head)[:, :, :keep]
    return jnp.pad(w, ((0, 0), (0, 0), (0, LANES - keep))).reshape(r, HEAD_W)


def kernel(x, positions, g_mix_pre, w_in, g_q_lat, w_uq, g_kv_lat, w_ukv, w_o_attn,
           w_pool_group, pool_scale, w_o_pool, w_out, g_mix_post, g_ffn_pre,
           w_gate_up, w_down, g_ffn_post):
    b, s, d = x.shape
    assert b == 1 and d == D_MODEL
    assert s % max(PRE_TM, ATTN_TQ, ATTN_TK, MIX_TM, FFN_TM) == 0
    vec = lambda g: g.reshape(1, -1).astype(F32)

    o_kr = Q_LORA_RANK + KV_LORA_RANK
    zeros = lambda n: jnp.zeros((D_MODEL, n), BF16)
    w_in_b = w_in.astype(BF16)
    w_in_p = jnp.concatenate(
        [w_in_b[:, :o_kr], zeros(QK_NOPE_DIM), w_in_b[:, o_kr:o_kr + QK_ROPE_DIM],
         zeros(LANES - QK_HEAD_DIM), w_in_b[:, o_kr + QK_ROPE_DIM:]], axis=1)
    w_uqt = _head_tiles(w_uq, QK_HEAD_DIM, QK_HEAD_DIM).T.astype(BF16)
    w_uk_p = _head_tiles(w_ukv, QK_NOPE_DIM + V_HEAD_DIM, QK_NOPE_DIM).astype(BF16)
    w_uv = w_ukv.reshape(KV_LORA_RANK, N_HEADS, QK_NOPE_DIM + V_HEAD_DIM)[:, :, QK_NOPE_DIM:]
    w_uvt = w_uv.reshape(KV_LORA_RANK, N_HEADS * V_HEAD_DIM).T.astype(BF16)

    inv = 1.0 / (ROPE_BASE ** (jnp.arange(0, QK_ROPE_DIM, 2, dtype=F32) / QK_ROPE_DIM))
    inv_rows = inv.reshape(QK_ROPE_DIM // 2, 1)

    x2 = x.reshape(s, d)
    q, k, v, u, ga, gb, qn, kn = _pre_call(
        x2, positions.reshape(1, s), vec(g_mix_pre), w_in_p, vec(g_q_lat), w_uqt,
        vec(g_kv_lat), w_uk_p, w_uvt, inv_rows)
    attn = _attn_call(q, k, v, qn, kn)
    x1 = _mix_call(x2, attn, u, ga, gb, w_o_attn.astype(BF16), w_pool_group.astype(BF16),
                   vec(pool_scale), w_o_pool.astype(BF16), w_out.astype(BF16), vec(g_mix_post))
    out = _ffn_call(x1, vec(g_ffn_pre), w_gate_up.astype(BF16), w_down.astype(BF16),
                    vec(g_ffn_post))
    return out.reshape(b, s, d)
```

```python
import functools
import math

import jax
import jax.numpy as jnp
from jax import lax
from jax.experimental import pallas as pl
from jax.experimental.pallas import tpu as pltpu

D_MODEL = 1024
N_HEADS = 8
QK_NOPE_DIM = 64
QK_ROPE_DIM = 32
V_HEAD_DIM = 64
Q_LORA_RANK = 384
KV_LORA_RANK = 256
QK_HEAD_DIM = QK_NOPE_DIM + QK_ROPE_DIM
ROPE_BASE = 10000.0
POOL_WINDOWS = (2, 4, 8, 16)
POOL_GROUP_DIM = 128
POOL_WIDTH = len(POOL_WINDOWS) * POOL_GROUP_DIM
D_FF = 2816
NORM_EPS = 1e-6

LANES = 128
HALO = 8
HEAD_W = N_HEADS * LANES
VMEM_LIMIT = 56 * 1024 * 1024

Q_SCALE = (QK_HEAD_DIM ** -0.5) * math.log2(math.e)

PRE_TM = 512
ATTN_TQ = 1024
ATTN_TK = 2048
QK_LEAD = 1
PV_LAG = 1
MXU_DIM = 256
UNIT_LEAD = 6
EXP_HEADROOM = 60.0
BF16_ROWS = 16
VT_ROWS = V_HEAD_DIM + BF16_ROWS
MIX_TM = 512
FFN_TM = 512

F32 = jnp.float32
BF16 = jnp.bfloat16


def _rms(x, g):
    return x * lax.rsqrt(jnp.mean(x * x, axis=-1, keepdims=True) + NORM_EPS) * g


def _dot(a, b):
    return jnp.dot(a, b, preferred_element_type=F32)


def _pre_kernel(x_ref, pos_ref, g_pre_ref, w_in_ref, g_q_ref, w_uqt_ref, g_kv_ref,
                w_uk_ref, w_uvt_ref, inv_ref,
                q_ref, k_ref, v_ref, u_ref, ga_ref, gb_ref, qn_ref, kn_ref):
    h = _rms(x_ref[...], g_pre_ref[...]).astype(BF16)
    p = _dot(h, w_in_ref[...])
    o = 0
    c_q = p[:, o:o + Q_LORA_RANK]; o += Q_LORA_RANK
    c_kv = p[:, o:o + KV_LORA_RANK]; o += KV_LORA_RANK
    kr = p[:, o:o + LANES]; o += LANES
    u_ref[...] = p[:, o:o + POOL_WIDTH]; o += POOL_WIDTH
    ga_ref[...] = jax.nn.sigmoid(p[:, o:o + D_MODEL]); o += D_MODEL
    gb_ref[...] = jax.nn.sigmoid(p[:, o:o + D_MODEL])

    half = QK_ROPE_DIM // 2
    ang = inv_ref[...] * pos_ref[...].astype(F32)
    cos = jnp.cos(ang)
    sin = jnp.sin(ang)

    def rope_t(t):
        t1 = t[QK_NOPE_DIM:QK_NOPE_DIM + half]
        t2 = t[QK_NOPE_DIM + half:QK_HEAD_DIM]
        return jnp.concatenate(
            [t[:QK_NOPE_DIM], t1 * cos - t2 * sin, t2 * cos + t1 * sin, t[QK_HEAD_DIM:]], axis=0)

    cq_t = _rms(c_q, g_q_ref[...]).T.astype(BF16)
    ckv_n = _rms(c_kv, g_kv_ref[...])
    q_t = _dot(w_uqt_ref[...], cq_t)
    vt = _dot(w_uvt_ref[...], ckv_n.T.astype(BF16))
    k = _dot(ckv_n.astype(BF16), w_uk_ref[...])
    kr = rope_t(kr.T).T
    for hd in range(N_HEADS):
        sl = slice(hd * LANES, (hd + 1) * LANES)
        qt = (rope_t(q_t[sl, :]) * Q_SCALE).astype(BF16)
        kb = (k[:, sl] + kr).astype(BF16)
        q_ref[hd] = qt
        k_ref[hd] = kb
        qf = qt.astype(F32)
        kf = kb.astype(F32)
        qn_ref[hd:hd + 1, :] = jnp.sum(qf * qf, axis=0, keepdims=True)
        k_sq = jnp.max(jnp.sum(kf * kf, axis=1, keepdims=True), axis=0, keepdims=True)
        kn_ref[0, hd:hd + 1, :] = jnp.broadcast_to(k_sq, (1, LANES))
    pad_row = lax.broadcasted_iota(jnp.int32, (VT_ROWS - V_HEAD_DIM, vt.shape[1]), 0)
    ones_pad = jnp.where(pad_row == 0, 1.0, 0.0).astype(BF16)
    for hd in range(N_HEADS):
        lo = hd * VT_ROWS
        v_ref[lo:lo + V_HEAD_DIM, :] = vt[hd * V_HEAD_DIM:(hd + 1) * V_HEAD_DIM, :].astype(BF16)
        v_ref[lo + V_HEAD_DIM:lo + VT_ROWS, :] = ones_pad


def _pre_call(x, pos, g_pre, w_in_p, g_q, w_uqt, g_kv, w_uk_p, w_uvt, inv_rows):
    s = x.shape[0]
    tm = PRE_TM
    row = lambda w: pl.BlockSpec((tm, w), lambda i: (i, 0))
    full = lambda a: pl.BlockSpec(a.shape, lambda i: (0,) * a.ndim)
    heads = lambda n: pl.BlockSpec((n, tm, LANES), lambda i: (0, i, 0))
    return pl.pallas_call(
        _pre_kernel,
        grid=(s // tm,),
        in_specs=[row(D_MODEL), pl.BlockSpec((1, tm), lambda i: (0, i)), full(g_pre),
                  full(w_in_p), full(g_q), full(w_uqt), full(g_kv), full(w_uk_p), full(w_uvt),
                  full(inv_rows)],
        out_specs=[pl.BlockSpec((N_HEADS, LANES, tm), lambda i: (0, 0, i)), heads(N_HEADS),
                   pl.BlockSpec((N_HEADS * VT_ROWS, tm), lambda i: (0, i)),
                   row(POOL_WIDTH), row(D_MODEL), row(D_MODEL),
                   pl.BlockSpec((N_HEADS, tm), lambda i: (0, i)),
                   pl.BlockSpec((1, N_HEADS, LANES), lambda i: (i, 0, 0))],
        out_shape=[jax.ShapeDtypeStruct((N_HEADS, LANES, s), BF16),
                   jax.ShapeDtypeStruct((N_HEADS, s, LANES), BF16),
                   jax.ShapeDtypeStruct((N_HEADS * VT_ROWS, s), BF16),
                   jax.ShapeDtypeStruct((s, POOL_WIDTH), F32),
                   jax.ShapeDtypeStruct((s, D_MODEL), F32),
                   jax.ShapeDtypeStruct((s, D_MODEL), F32),
                   jax.ShapeDtypeStruct((N_HEADS, s), F32),
                   jax.ShapeDtypeStruct((s // tm, N_HEADS, LANES), F32)],
        compiler_params=pltpu.CompilerParams(
            dimension_semantics=("parallel",), vmem_limit_bytes=VMEM_LIMIT),
        name="pre_proj",
    )(x, pos, g_pre, w_in_p, g_q, w_uqt, g_kv, w_uk_p, w_uvt, inv_rows)


def _attn_kernel(qt_ref, k_ref, vt_ref, qn_ref, kn_ref, kn_next_ref, o_ref, m_ref, acc_ref, flag_ref):
    j = pl.program_id(1)

    def head_rows(hd):
        return slice(hd * VT_ROWS, (hd + 1) * VT_ROWS)

    def within_headroom(kn_blk_ref):
        k_norm = jnp.sqrt(jnp.max(kn_blk_ref[...], axis=0)[:, :1])
        bound = jnp.sqrt(qn_ref[...]) * k_norm
        excess = jnp.max(bound - m_ref[...].reshape(bound.shape))
        return (excess <= EXP_HEADROOM).astype(jnp.int32)

    @pl.when(j == 0)
    def _init():
        for hd in range(N_HEADS):
            m_ref[hd] = _dot(k_ref[hd, 0:BF16_ROWS, :], qt_ref[hd])[0:1, :]
        acc_ref[...] = jnp.zeros(acc_ref.shape, F32)
        flag_ref[0] = within_headroom(kn_ref)

    streaming = flag_ref[0] == 1

    @pl.when(streaming)
    def _stream():
        flag_ref[0] = within_headroom(kn_next_ref)
        tk, tq = k_ref.shape[1], qt_ref.shape[2]
        units = [(hd, qc, kc) for qc in range(0, tq, MXU_DIM) for kc in range(0, tk, MXU_DIM)
                 for hd in range(N_HEADS)]
        scores, pvs = {}, {}
        for t in range(len(units) + UNIT_LEAD):
            if t < len(units):
                hd, qc, kc = units[t]
                scores[t] = _dot(k_ref[hd, kc:kc + MXU_DIM, :], qt_ref[hd, :, qc:qc + MXU_DIM])
            u = t - UNIT_LEAD
            if u >= 0:
                hd, qc, kc = units[u]
                rows = head_rows(hd)
                cols = slice(qc, qc + MXU_DIM)
                p = jnp.exp2(scores.pop(u) - m_ref[hd, :, cols]).astype(BF16)
                d = _dot(vt_ref[rows, kc:kc + MXU_DIM], p)
                pvs[hd, qc] = d if (hd, qc) not in pvs else pvs[hd, qc] + d
                if kc + MXU_DIM == tk:
                    acc_ref[rows, cols] = acc_ref[rows, cols] + pvs.pop((hd, qc))

    @pl.when(jnp.logical_not(streaming))
    def _two_pass():
        scores, probs, alphas = {}, {}, {}
        for t in range(N_HEADS + QK_LEAD + PV_LAG):
            if t < N_HEADS:
                scores[t] = _dot(k_ref[t], qt_ref[t])
            hd = t - QK_LEAD
            if 0 <= hd < N_HEADS:
                s = scores.pop(hd)
                m_prev = m_ref[hd]
                m_new = jnp.maximum(m_prev, jnp.max(s, axis=0, keepdims=True))
                alphas[hd] = jnp.exp2(m_prev - m_new)
                probs[hd] = jnp.exp2(s - m_new).astype(BF16)
                m_ref[hd] = m_new
            hd = t - QK_LEAD - PV_LAG
            if 0 <= hd < N_HEADS:
                rows = head_rows(hd)
                pv = _dot(vt_ref[rows, :], probs.pop(hd))
                acc_ref[rows, :] = alphas.pop(hd) * acc_ref[rows, :] + pv
        flag_ref[0] = within_headroom(kn_next_ref)

    @pl.when(j == pl.num_programs(1) - 1)
    def _finish():
        outs = []
        for hd in range(N_HEADS):
            lo = hd * VT_ROWS
            denom = acc_ref[lo + V_HEAD_DIM:lo + V_HEAD_DIM + 1, :]
            outs.append(acc_ref[lo:lo + V_HEAD_DIM, :] / denom)
        o_ref[...] = jnp.concatenate(outs, axis=0).T.astype(o_ref.dtype)


def _attn_call(qt, k, vt, qn, kn):
    s = k.shape[1]
    tq, tk = ATTN_TQ, ATTN_TK
    width = N_HEADS * V_HEAD_DIM
    kn_rows = tk // PRE_TM
    last_kv = s // tk - 1
    return pl.pallas_call(
        _attn_kernel,
        grid=(s // tq, s // tk),
        in_specs=[pl.BlockSpec((N_HEADS, LANES, tq), lambda i, j: (0, 0, i)),
                  pl.BlockSpec((N_HEADS, tk, LANES), lambda i, j: (0, j, 0)),
                  pl.BlockSpec((N_HEADS * VT_ROWS, tk), lambda i, j: (0, j)),
                  pl.BlockSpec((N_HEADS, tq), lambda i, j: (0, i)),
                  pl.BlockSpec((kn_rows, N_HEADS, LANES), lambda i, j: (j, 0, 0)),
                  pl.BlockSpec((kn_rows, N_HEADS, LANES),
                               lambda i, j: (jnp.minimum(j + 1, last_kv), 0, 0))],
        out_specs=pl.BlockSpec((tq, width), lambda i, j: (i, 0)),
        out_shape=jax.ShapeDtypeStruct((s, width), BF16),
        scratch_shapes=[pltpu.VMEM((N_HEADS, 1, tq), F32),
                        pltpu.VMEM((N_HEADS * VT_ROWS, tq), F32),
                        pltpu.SMEM((1,), jnp.int32)],
        compiler_params=pltpu.CompilerParams(
            dimension_semantics=("parallel", "arbitrary"), vmem_limit_bytes=VMEM_LIMIT),
        name="mla_attention",
    )(qt, k, vt, qn, kn, kn)


def _mix_kernel(x_ref, attn_ref, u_ref, up_ref, un_ref, ga_ref, gb_ref,
                w_oa_ref, w_pg_ref, pscale_ref, w_ob_ref, w_out_ref, g_post_ref,
                o_ref, ext_ref, *, seq_len):
    i = pl.program_id(0)
    tm = u_ref.shape[0]
    ext_ref[0:HALO, :] = jnp.where(i > 0, up_ref[...], 0.0)
    ext_ref[HALO:HALO + tm, :] = u_ref[...]
    ext_ref[HALO + tm:, :] = jnp.where(i < pl.num_programs(0) - 1, un_ref[...], 0.0)

    t = i * tm + lax.broadcasted_iota(jnp.int32, (tm, 1), 0)
    mixed = []
    for g, w in enumerate(POOL_WINDOWS):
        left = w // 2
        right = w - left - 1
        cols = slice(g * POOL_GROUP_DIM, (g + 1) * POOL_GROUP_DIM)
        win = ext_ref[HALO - left:HALO - left + tm, cols]
        for d in range(-left + 1, right + 1):
            win = win + ext_ref[HALO + d:HALO + d + tm, cols]
        cnt = (jnp.minimum(t + right, seq_len - 1) - jnp.maximum(t - left, 0) + 1).astype(F32)
        pooled = win / cnt - u_ref[:, cols]
        mixed.append(_dot(pooled.astype(BF16), w_pg_ref[g]))
    pool = jnp.concatenate(mixed, axis=1) * pscale_ref[...]

    a = _dot(attn_ref[...], w_oa_ref[...])
    b = _dot(pool.astype(BF16), w_ob_ref[...])
    merged = ga_ref[...] * a + gb_ref[...] * b
    y = _dot(merged.astype(BF16), w_out_ref[...])
    o_ref[...] = x_ref[...] + _rms(y, g_post_ref[...])


def _mix_call(x, attn, u, ga, gb, w_oa, w_pg, pscale, w_ob, w_out, g_post):
    s = x.shape[0]
    tm = MIX_TM
    nb = tm // HALO
    last = s // HALO - 1
    row = lambda w: pl.BlockSpec((tm, w), lambda i: (i, 0))
    full = lambda a: pl.BlockSpec(a.shape, lambda i: (0,) * a.ndim)
    prev = pl.BlockSpec((HALO, POOL_WIDTH), lambda i: (jnp.maximum(i * nb - 1, 0), 0))
    nxt = pl.BlockSpec((HALO, POOL_WIDTH), lambda i: (jnp.minimum((i + 1) * nb, last), 0))
    return pl.pallas_call(
        functools.partial(_mix_kernel, seq_len=s),
        grid=(s // tm,),
        in_specs=[row(D_MODEL), row(POOL_WIDTH), row(POOL_WIDTH), prev, nxt,
                  row(D_MODEL), row(D_MODEL),
                  full(w_oa), full(w_pg), full(pscale), full(w_ob), full(w_out), full(g_post)],
        out_specs=row(D_MODEL),
        out_shape=jax.ShapeDtypeStruct((s, D_MODEL), F32),
        scratch_shapes=[pltpu.VMEM((tm + 2 * HALO, POOL_WIDTH), F32)],
        compiler_params=pltpu.CompilerParams(
            dimension_semantics=("parallel",), vmem_limit_bytes=VMEM_LIMIT),
        name="mix_merge",
    )(x, attn, u, u, u, ga, gb, w_oa, w_pg, pscale, w_ob, w_out, g_post)


def _ffn_kernel(x_ref, g_pre_ref, w_gu_ref, w_dn_ref, g_post_ref, o_ref):
    x = x_ref[...]
    hf = _rms(x, g_pre_ref[...]).astype(BF16)
    gu = _dot(hf, w_gu_ref[...])
    act = jax.nn.silu(gu[:, :D_FF]) * gu[:, D_FF:]
    ff = _dot(act.astype(BF16), w_dn_ref[...])
    o_ref[...] = x + _rms(ff, g_post_ref[...])


def _ffn_call(x, g_pre, w_gu, w_dn, g_post):
    s = x.shape[0]
    tm = FFN_TM
    row = pl.BlockSpec((tm, D_MODEL), lambda i: (i, 0))
    full = lambda a: pl.BlockSpec(a.shape, lambda i: (0,) * a.ndim, pipeline_mode=pl.Buffered(1))
    return pl.pallas_call(
        _ffn_kernel,
        grid=(s // tm,),
        in_specs=[row, full(g_pre), full(w_gu), full(w_dn), full(g_post)],
        out_specs=row,
        out_shape=jax.ShapeDtypeStruct((s, D_MODEL), F32),
        compiler_params=pltpu.CompilerParams(
            dimension_semantics=("parallel",), vmem_limit_bytes=VMEM_LIMIT),
        name="swiglu_ffn",
    )(x, g_pre, w_gu, w_dn, g_post)


def _head_tiles(w, per_head, keep):
    r = w.shape[0]
    w = w.reshape(r, N_HEADS, per_head)[:, :, :keep]
    return jnp.pad(w, ((0, 0), (0, 0), (0, LANES - keep))).reshape(r, HEAD_W)


def kernel(x, positions, g_mix_pre, w_in, g_q_lat, w_uq, g_kv_lat, w_ukv, w_o_attn,
           w_pool_group, pool_scale, w_o_pool, w_out, g_mix_post, g_ffn_pre,
           w_gate_up, w_down, g_ffn_post):
    b, s, d = x.shape
    assert b == 1 and d == D_MODEL
    assert s % max(PRE_TM, ATTN_TQ, ATTN_TK, MIX_TM, FFN_TM) == 0
    vec = lambda g: g.reshape(1, -1).astype(F32)

    o_kr = Q_LORA_RANK + KV_LORA_RANK
    zeros = lambda n: jnp.zeros((D_MODEL, n), BF16)
    w_in_b = w_in.astype(BF16)
    w_in_p = jnp.concatenate(
        [w_in_b[:, :o_kr], zeros(QK_NOPE_DIM), w_in_b[:, o_kr:o_kr + QK_ROPE_DIM],
         zeros(LANES - QK_HEAD_DIM), w_in_b[:, o_kr + QK_ROPE_DIM:]], axis=1)
    w_uqt = _head_tiles(w_uq, QK_HEAD_DIM, QK_HEAD_DIM).T.astype(BF16)
    w_uk_p = _head_tiles(w_ukv, QK_NOPE_DIM + V_HEAD_DIM, QK_NOPE_DIM).astype(BF16)
    w_uv = w_ukv.reshape(KV_LORA_RANK, N_HEADS, QK_NOPE_DIM + V_HEAD_DIM)[:, :, QK_NOPE_DIM:]
    w_uvt = w_uv.reshape(KV_LORA_RANK, N_HEADS * V_HEAD_DIM).T.astype(BF16)

    inv = 1.0 / (ROPE_BASE ** (jnp.arange(0, QK_ROPE_DIM, 2, dtype=F32) / QK_ROPE_DIM))
    inv_rows = inv.reshape(QK_ROPE_DIM // 2, 1)

    x2 = x.reshape(s, d)
    q, k, v, u, ga, gb, qn, kn = _pre_call(
        x2, positions.reshape(1, s), vec(g_mix_pre), w_in_p, vec(g_q_lat), w_uqt,
        vec(g_kv_lat), w_uk_p, w_uvt, inv_rows)
    attn = _attn_call(q, k, v, qn, kn)
    x1 = _mix_call(x2, attn, u, ga, gb, w_o_attn.astype(BF16), w_pool_group.astype(BF16),
                   vec(pool_scale), w_o_pool.astype(BF16), w_out.astype(BF16), vec(g_mix_post))
    out = _ffn_call(x1, vec(g_ffn_pre), w_gate_up.astype(BF16), w_down.astype(BF16),
                    vec(g_ffn_post))
    return out.reshape(b, s, d)
```

```python
import functools
import math

import jax
import jax.numpy as jnp
from jax import lax
from jax.experimental import pallas as pl
from jax.experimental.pallas import tpu as pltpu

D_MODEL = 1024
N_HEADS = 8
QK_NOPE_DIM = 64
QK_ROPE_DIM = 32
V_HEAD_DIM = 64
Q_LORA_RANK = 384
KV_LORA_RANK = 256
QK_HEAD_DIM = QK_NOPE_DIM + QK_ROPE_DIM
ROPE_BASE = 10000.0
POOL_WINDOWS = (2, 4, 8, 16)
POOL_GROUP_DIM = 128
POOL_WIDTH = len(POOL_WINDOWS) * POOL_GROUP_DIM
D_FF = 2816
NORM_EPS = 1e-6

LANES = 128
HALO = 8
HEAD_W = N_HEADS * LANES
VMEM_LIMIT = 56 * 1024 * 1024

Q_SCALE = (QK_HEAD_DIM ** -0.5) * math.log2(math.e)

PRE_TM = 512
ATTN_TQ = 1024
ATTN_TK = 2048
QK_LEAD = 1
PV_LAG = 1
MXU_DIM = 256
UNIT_LEAD = 6
EXP_HEADROOM = 60.0
BF16_ROWS = 16
VT_ROWS = V_HEAD_DIM + BF16_ROWS
MIX_TM = 512
FFN_TM = 512

F32 = jnp.float32
BF16 = jnp.bfloat16


def _rms(x, g):
    return x * lax.rsqrt(jnp.mean(x * x, axis=-1, keepdims=True) + NORM_EPS) * g


def _dot(a, b):
    return jnp.dot(a, b, preferred_element_type=F32)


def _pre_kernel(x_ref, pos_ref, g_pre_ref, w_in_ref, g_q_ref, w_uqt_ref, g_kv_ref,
                w_uk_ref, w_uvt_ref, inv_ref,
                q_ref, k_ref, v_ref, u_ref, ga_ref, gb_ref, qn_ref, kn_ref):
    h = _rms(x_ref[...], g_pre_ref[...]).astype(BF16)
    p = _dot(h, w_in_ref[...])
    o = 0
    c_q = p[:, o:o + Q_LORA_RANK]; o += Q_LORA_RANK
    c_kv = p[:, o:o + KV_LORA_RANK]; o += KV_LORA_RANK
    kr = p[:, o:o + LANES]; o += LANES
    u_ref[...] = p[:, o:o + POOL_WIDTH]; o += POOL_WIDTH
    ga_ref[...] = jax.nn.sigmoid(p[:, o:o + D_MODEL]); o += D_MODEL
    gb_ref[...] = jax.nn.sigmoid(p[:, o:o + D_MODEL])

    half = QK_ROPE_DIM // 2
    ang = inv_ref[...] * pos_ref[...].astype(F32)
    cos = jnp.cos(ang)
    sin = jnp.sin(ang)

    def rope_t(t):
        t1 = t[QK_NOPE_DIM:QK_NOPE_DIM + half]
        t2 = t[QK_NOPE_DIM + half:QK_HEAD_DIM]
        return jnp.concatenate(
            [t[:QK_NOPE_DIM], t1 * cos - t2 * sin, t2 * cos + t1 * sin, t[QK_HEAD_DIM:]], axis=0)

    cq_t = _rms(c_q, g_q_ref[...]).T.astype(BF16)
    ckv_n = _rms(c_kv, g_kv_ref[...])
    q_t = _dot(w_uqt_ref[...], cq_t)
    vt = _dot(w_uvt_ref[...], ckv_n.T.astype(BF16))
    k = _dot(ckv_n.astype(BF16), w_uk_ref[...])
    kr = rope_t(kr.T).T
    for hd in range(N_HEADS):
        sl = slice(hd * LANES, (hd + 1) * LANES)
        qt = (rope_t(q_t[sl, :]) * Q_SCALE).astype(BF16)
        kb = (k[:, sl] + kr).astype(BF16)
        q_ref[hd] = qt
        k_ref[hd] = kb
        qf = qt.astype(F32)
        kf = kb.astype(F32)
        qn_ref[hd:hd + 1, :] = jnp.sum(qf * qf, axis=0, keepdims=True)
        k_sq = jnp.max(jnp.sum(kf * kf, axis=1, keepdims=True), axis=0, keepdims=True)
        kn_ref[0, hd:hd + 1, :] = jnp.broadcast_to(k_sq, (1, LANES))
    pad_row = lax.broadcasted_iota(jnp.int32, (VT_ROWS - V_HEAD_DIM, vt.shape[1]), 0)
    ones_pad = jnp.where(pad_row == 0, 1.0, 0.0).astype(BF16)
    for hd in range(N_HEADS):
        lo = hd * VT_ROWS
        v_ref[lo:lo + V_HEAD_DIM, :] = vt[hd * V_HEAD_DIM:(hd + 1) * V_HEAD_DIM, :].astype(BF16)
        v_ref[lo + V_HEAD_DIM:lo + VT_ROWS, :] = ones_pad


def _pre_call(x, pos, g_pre, w_in_p, g_q, w_uqt, g_kv, w_uk_p, w_uvt, inv_rows):
    s = x.shape[0]
    tm = PRE_TM
    row = lambda w: pl.BlockSpec((tm, w), lambda i: (i, 0))
    full = lambda a: pl.BlockSpec(a.shape, lambda i: (0,) * a.ndim)
    heads = lambda n: pl.BlockSpec((n, tm, LANES), lambda i: (0, i, 0))
    return pl.pallas_call(
        _pre_kernel,
        grid=(s // tm,),
        in_specs=[row(D_MODEL), pl.BlockSpec((1, tm), lambda i: (0, i)), full(g_pre),
                  full(w_in_p), full(g_q), full(w_uqt), full(g_kv), full(w_uk_p), full(w_uvt),
                  full(inv_rows)],
        out_specs=[pl.BlockSpec((N_HEADS, LANES, tm), lambda i: (0, 0, i)), heads(N_HEADS),
                   pl.BlockSpec((N_HEADS * VT_ROWS, tm), lambda i: (0, i)),
                   row(POOL_WIDTH), row(D_MODEL), row(D_MODEL),
                   pl.BlockSpec((N_HEADS, tm), lambda i: (0, i)),
                   pl.BlockSpec((1, N_HEADS, LANES), lambda i: (i, 0, 0))],
        out_shape=[jax.ShapeDtypeStruct((N_HEADS, LANES, s), BF16),
                   jax.ShapeDtypeStruct((N_HEADS, s, LANES), BF16),
                   jax.ShapeDtypeStruct((N_HEADS * VT_ROWS, s), BF16),
                   jax.ShapeDtypeStruct((s, POOL_WIDTH), F32),
                   jax.ShapeDtypeStruct((s, D_MODEL), F32),
                   jax.ShapeDtypeStruct((s, D_MODEL), F32),
                   jax.ShapeDtypeStruct((N_HEADS, s), F32),
                   jax.ShapeDtypeStruct((s // tm, N_HEADS, LANES), F32)],
        compiler_params=pltpu.CompilerParams(
            dimension_semantics=("parallel",), vmem_limit_bytes=VMEM_LIMIT),
        name="pre_proj",
    )(x, pos, g_pre, w_in_p, g_q, w_uqt, g_kv, w_uk_p, w_uvt, inv_rows)


def _attn_kernel(qt_ref, k_ref, vt_ref, qn_ref, kn_ref, kn_next_ref, o_ref, m_ref, acc_ref, flag_ref):
    j = pl.program_id(1)

    def head_rows(hd):
        return slice(hd * VT_ROWS, (hd + 1) * VT_ROWS)

    def within_headroom(kn_blk_ref):
        k_norm = jnp.sqrt(jnp.max(kn_blk_ref[...], axis=0)[:, :1])
        bound = jnp.sqrt(qn_ref[...]) * k_norm
        excess = jnp.max(bound - m_ref[...].reshape(bound.shape))
        return (excess <= EXP_HEADROOM).astype(jnp.int32)

    @pl.when(j == 0)
    def _init():
        for hd in range(N_HEADS):
            m_ref[hd] = _dot(k_ref[hd, 0:BF16_ROWS, :], qt_ref[hd])[0:1, :]
        acc_ref[...] = jnp.zeros(acc_ref.shape, F32)
        flag_ref[0] = within_headroom(kn_ref)

    streaming = flag_ref[0] == 1

    @pl.when(streaming)
    def _stream():
        flag_ref[0] = within_headroom(kn_next_ref)
        tk, tq = k_ref.shape[1], qt_ref.shape[2]
        units = [(hd, qc, kc) for qc in range(0, tq, MXU_DIM) for kc in range(0, tk, MXU_DIM)
                 for hd in range(N_HEADS)]
        scores, pvs = {}, {}
        for t in range(len(units) + UNIT_LEAD):
            if t < len(units):
                hd, qc, kc = units[t]
                scores[t] = _dot(k_ref[hd, kc:kc + MXU_DIM, :], qt_ref[hd, :, qc:qc + MXU_DIM])
            u = t - UNIT_LEAD
            if u >= 0:
                hd, qc, kc = units[u]
                rows = head_rows(hd)
                cols = slice(qc, qc + MXU_DIM)
                p = jnp.exp2(scores.pop(u) - m_ref[hd, :, cols]).astype(BF16)
                d = _dot(vt_ref[rows, kc:kc + MXU_DIM], p)
                pvs[hd, qc] = d if (hd, qc) not in pvs else pvs[hd, qc] + d
                if kc + MXU_DIM == tk:
                    acc_ref[rows, cols] = acc_ref[rows, cols] + pvs.pop((hd, qc))

    @pl.when(jnp.logical_not(streaming))
    def _two_pass():
        scores, probs, alphas = {}, {}, {}
        for t in range(N_HEADS + QK_LEAD + PV_LAG):
            if t < N_HEADS:
                scores[t] = _dot(k_ref[t], qt_ref[t])
            hd = t - QK_LEAD
            if 0 <= hd < N_HEADS:
                s = scores.pop(hd)
                m_prev = m_ref[hd]
                m_new = jnp.maximum(m_prev, jnp.max(s, axis=0, keepdims=True))
                alphas[hd] = jnp.exp2(m_prev - m_new)
                probs[hd] = jnp.exp2(s - m_new).astype(BF16)
                m_ref[hd] = m_new
            hd = t - QK_LEAD - PV_LAG
            if 0 <= hd < N_HEADS:
                rows = head_rows(hd)
                pv = _dot(vt_ref[rows, :], probs.pop(hd))
                acc_ref[rows, :] = alphas.pop(hd) * acc_ref[rows, :] + pv
        flag_ref[0] = within_headroom(kn_next_ref)

    @pl.when(j == pl.num_programs(1) - 1)
    def _finish():
        outs = []
        for hd in range(N_HEADS):
            lo = hd * VT_ROWS
            denom = acc_ref[lo + V_HEAD_DIM:lo + V_HEAD_DIM + 1, :]
            outs.append(acc_ref[lo:lo + V_HEAD_DIM, :] / denom)
        o_ref[...] = jnp.concatenate(outs, axis=0).T.astype(o_ref.dtype)


def _attn_call(qt, k, vt, qn, kn):
    s = k.shape[1]
    tq, tk = ATTN_TQ, ATTN_TK
    width = N_HEADS * V_HEAD_DIM
    kn_rows = tk // PRE_TM
    last_kv = s // tk - 1
    return pl.pallas_call(
        _attn_kernel,
        grid=(s // tq, s // tk),
        in_specs=[pl.BlockSpec((N_HEADS, LANES, tq), lambda i, j: (0, 0, i)),
                  pl.BlockSpec((N_HEADS, tk, LANES), lambda i, j: (0, j, 0)),
                  pl.BlockSpec((N_HEADS * VT_ROWS, tk), lambda i, j: (0, j)),
                  pl.BlockSpec((N_HEADS, tq), lambda i, j: (0, i)),
                  pl.BlockSpec((kn_rows, N_HEADS, LANES), lambda i, j: (j, 0, 0)),
                  pl.BlockSpec((kn_rows, N_HEADS, LANES),
                               lambda i, j: (jnp.minimum(j + 1, last_kv), 0, 0))],
        out_specs=pl.BlockSpec((tq, width), lambda i, j: (i, 0)),
        out_shape=jax.ShapeDtypeStruct((s, width), BF16),
        scratch_shapes=[pltpu.VMEM((N_HEADS, 1, tq), F32),
                        pltpu.VMEM((N_HEADS * VT_ROWS, tq), F32),
                        pltpu.SMEM((1,), jnp.int32)],
        compiler_params=pltpu.CompilerParams(
            dimension_semantics=("parallel", "arbitrary"), vmem_limit_bytes=VMEM_LIMIT),
        name="mla_attention",
    )(qt, k, vt, qn, kn, kn)


def _mix_kernel(x_ref, attn_ref, u_ref, up_ref, un_ref, ga_ref, gb_ref,
                w_oa_ref, w_pg_ref, pscale_ref, w_ob_ref, w_out_ref, g_post_ref,
                o_ref, ext_ref, *, seq_len):
    i = pl.program_id(0)
    tm = u_ref.shape[0]
    ext_ref[0:HALO, :] = jnp.where(i > 0, up_ref[...], 0.0)
    ext_ref[HALO:HALO + tm, :] = u_ref[...]
    ext_ref[HALO + tm:, :] = jnp.where(i < pl.num_programs(0) - 1, un_ref[...], 0.0)

    t = i * tm + lax.broadcasted_iota(jnp.int32, (tm, 1), 0)
    n_ext = tm + 2 * HALO
    mixed = []
    for g, w in enumerate(POOL_WINDOWS):
        left = w // 2
        right = w - left - 1
        cols = slice(g * POOL_GROUP_DIM, (g + 1) * POOL_GROUP_DIM)
        acc = ext_ref[:, cols]
        span = 1
        while span < w:
            acc = acc + pltpu.roll(acc, n_ext - span, axis=0)
            span *= 2
        win = pltpu.roll(acc, n_ext - (HALO - left), axis=0)[:tm]
        cnt = (jnp.minimum(t + right, seq_len - 1) - jnp.maximum(t - left, 0) + 1).astype(F32)
        pooled = win / cnt - u_ref[:, cols]
        mixed.append(_dot(pooled.astype(BF16), w_pg_ref[g]))
    pool = jnp.concatenate(mixed, axis=1) * pscale_ref[...]

    a = _dot(attn_ref[...], w_oa_ref[...])
    b = _dot(pool.astype(BF16), w_ob_ref[...])
    merged = ga_ref[...] * a + gb_ref[...] * b
    y = _dot(merged.astype(BF16), w_out_ref[...])
    o_ref[...] = x_ref[...] + _rms(y, g_post_ref[...])


def _mix_call(x, attn, u, ga, gb, w_oa, w_pg, pscale, w_ob, w_out, g_post):
    s = x.shape[0]
    tm = MIX_TM
    nb = tm // HALO
    last = s // HALO - 1
    row = lambda w: pl.BlockSpec((tm, w), lambda i: (i, 0))
    full = lambda a: pl.BlockSpec(a.shape, lambda i: (0,) * a.ndim)
    prev = pl.BlockSpec((HALO, POOL_WIDTH), lambda i: (jnp.maximum(i * nb - 1, 0), 0))
    nxt = pl.BlockSpec((HALO, POOL_WIDTH), lambda i: (jnp.minimum((i + 1) * nb, last), 0))
    return pl.pallas_call(
        functools.partial(_mix_kernel, seq_len=s),
        grid=(s // tm,),
        in_specs=[row(D_MODEL), row(POOL_WIDTH), row(POOL_WIDTH), prev, nxt,
                  row(D_MODEL), row(D_MODEL),
                  full(w_oa), full(w_pg), full(pscale), full(w_ob), full(w_out), full(g_post)],
        out_specs=row(D_MODEL),
        out_shape=jax.ShapeDtypeStruct((s, D_MODEL), F32),
        scratch_shapes=[pltpu.VMEM((tm + 2 * HALO, POOL_WIDTH), F32)],
        compiler_params=pltpu.CompilerParams(
            dimension_semantics=("parallel",), vmem_limit_bytes=VMEM_LIMIT),
        name="mix_merge",
    )(x, attn, u, u, u, ga, gb, w_oa, w_pg, pscale, w_ob, w_out, g_post)


def _ffn_kernel(x_ref, g_pre_ref, w_gu_ref, w_dn_ref, g_post_ref, o_ref):
    x = x_ref[...]
    hf = _rms(x, g_pre_ref[...]).astype(BF16)
    gu = _dot(hf, w_gu_ref[...])
    act = jax.nn.silu(gu[:, :D_FF]) * gu[:, D_FF:]
    ff = _dot(act.astype(BF16), w_dn_ref[...])
    o_ref[...] = x + _rms(ff, g_post_ref[...])


def _ffn_call(x, g_pre, w_gu, w_dn, g_post):
    s = x.shape[0]
    tm = FFN_TM
    row = pl.BlockSpec((tm, D_MODEL), lambda i: (i, 0))
    full = lambda a: pl.BlockSpec(a.shape, lambda i: (0,) * a.ndim, pipeline_mode=pl.Buffered(1))
    return pl.pallas_call(
        _ffn_kernel,
        grid=(s // tm,),
        in_specs=[row, full(g_pre), full(w_gu), full(w_dn), full(g_post)],
        out_specs=row,
        out_shape=jax.ShapeDtypeStruct((s, D_MODEL), F32),
        compiler_params=pltpu.CompilerParams(
            dimension_semantics=("parallel",), vmem_limit_bytes=VMEM_LIMIT),
        name="swiglu_ffn",
    )(x, g_pre, w_gu, w_dn, g_post)


def _head_tiles(w, per_head, keep):
    r = w.shape[0]
    w = w.reshape(r, N_HEADS, per_head)[:, :, :keep]
    return jnp.pad(w, ((0, 0), (0, 0), (0, LANES - keep))).reshape(r, HEAD_W)


def kernel(x, positions, g_mix_pre, w_in, g_q_lat, w_uq, g_kv_lat, w_ukv, w_o_attn,
           w_pool_group, pool_scale, w_o_pool, w_out, g_mix_post, g_ffn_pre,
           w_gate_up, w_down, g_ffn_post):
    b, s, d = x.shape
    assert b == 1 and d == D_MODEL
    assert s % max(PRE_TM, ATTN_TQ, ATTN_TK, MIX_TM, FFN_TM) == 0
    vec = lambda g: g.reshape(1, -1).astype(F32)

    o_kr = Q_LORA_RANK + KV_LORA_RANK
    zeros = lambda n: jnp.zeros((D_MODEL, n), BF16)
    w_in_b = w_in.astype(BF16)
    w_in_p = jnp.concatenate(
        [w_in_b[:, :o_kr], zeros(QK_NOPE_DIM), w_in_b[:, o_kr:o_kr + QK_ROPE_DIM],
         zeros(LANES - QK_HEAD_DIM), w_in_b[:, o_kr + QK_ROPE_DIM:]], axis=1)
    w_uqt = _head_tiles(w_uq, QK_HEAD_DIM, QK_HEAD_DIM).T.astype(BF16)
    w_uk_p = _head_tiles(w_ukv, QK_NOPE_DIM + V_HEAD_DIM, QK_NOPE_DIM).astype(BF16)
    w_uv = w_ukv.reshape(KV_LORA_RANK, N_HEADS, QK_NOPE_DIM + V_HEAD_DIM)[:, :, QK_NOPE_DIM:]
    w_uvt = w_uv.reshape(KV_LORA_RANK, N_HEADS * V_HEAD_DIM).T.astype(BF16)

    inv = 1.0 / (ROPE_BASE ** (jnp.arange(0, QK_ROPE_DIM, 2, dtype=F32) / QK_ROPE_DIM))
    inv_rows = inv.reshape(QK_ROPE_DIM // 2, 1)

    x2 = x.reshape(s, d)
    q, k, v, u, ga, gb, qn, kn = _pre_call(
        x2, positions.reshape(1, s), vec(g_mix_pre), w_in_p, vec(g_q_lat), w_uqt,
        vec(g_kv_lat), w_uk_p, w_uvt, inv_rows)
    attn = _attn_call(q, k, v, qn, kn)
    x1 = _mix_call(x2, attn, u, ga, gb, w_o_attn.astype(BF16), w_pool_group.astype(BF16),
                   vec(pool_scale), w_o_pool.astype(BF16), w_out.astype(BF16), vec(g_mix_post))
    out = _ffn_call(x1, vec(g_ffn_pre), w_gate_up.astype(BF16), w_down.astype(BF16),
                    vec(g_ffn_post))
    return out.reshape(b, s, d)
```

```python
import functools
import math

import jax
import jax.numpy as jnp
from jax import lax
from jax.experimental import pallas as pl
from jax.experimental.pallas import tpu as pltpu

D_MODEL = 1024
N_HEADS = 8
QK_NOPE_DIM = 64
QK_ROPE_DIM = 32
V_HEAD_DIM = 64
Q_LORA_RANK = 384
KV_LORA_RANK = 256
QK_HEAD_DIM = QK_NOPE_DIM + QK_ROPE_DIM
ROPE_BASE = 10000.0
POOL_WINDOWS = (2, 4, 8, 16)
POOL_GROUP_DIM = 128
POOL_WIDTH = len(POOL_WINDOWS) * POOL_GROUP_DIM
D_FF = 2816
NORM_EPS = 1e-6

LANES = 128
HALO = 8
HEAD_W = N_HEADS * LANES
VMEM_LIMIT = 56 * 1024 * 1024

Q_SCALE = (QK_HEAD_DIM ** -0.5) * math.log2(math.e)

PRE_TM = 512
ATTN_TQ = 1024
ATTN_TK = 2048
QK_LEAD = 1
PV_LAG = 1
MXU_DIM = 256
UNIT_LEAD = 6
EXP_HEADROOM = 60.0
BF16_ROWS = 16
VT_ROWS = V_HEAD_DIM + BF16_ROWS
MIX_TM = 256

F32 = jnp.float32
BF16 = jnp.bfloat16


def _rms(x, g):
    return x * lax.rsqrt(jnp.mean(x * x, axis=-1, keepdims=True) + NORM_EPS) * g


def _dot(a, b):
    return jnp.dot(a, b, preferred_element_type=F32)


def _pre_kernel(x_ref, pos_ref, g_pre_ref, w_in_ref, g_q_ref, w_uqt_ref, g_kv_ref,
                w_uk_ref, w_uvt_ref, inv_ref,
                q_ref, k_ref, v_ref, u_ref, ga_ref, gb_ref, qn_ref, kn_ref):
    h = _rms(x_ref[...], g_pre_ref[...]).astype(BF16)
    p = _dot(h, w_in_ref[...])
    o = 0
    c_q = p[:, o:o + Q_LORA_RANK]; o += Q_LORA_RANK
    c_kv = p[:, o:o + KV_LORA_RANK]; o += KV_LORA_RANK
    kr = p[:, o:o + LANES]; o += LANES
    u_ref[...] = p[:, o:o + POOL_WIDTH]; o += POOL_WIDTH
    ga_ref[...] = jax.nn.sigmoid(p[:, o:o + D_MODEL]); o += D_MODEL
    gb_ref[...] = jax.nn.sigmoid(p[:, o:o + D_MODEL])

    half = QK_ROPE_DIM // 2
    ang = inv_ref[...] * pos_ref[...].astype(F32)
    cos = jnp.cos(ang)
    sin = jnp.sin(ang)

    def rope_t(t):
        t1 = t[QK_NOPE_DIM:QK_NOPE_DIM + half]
        t2 = t[QK_NOPE_DIM + half:QK_HEAD_DIM]
        return jnp.concatenate(
            [t[:QK_NOPE_DIM], t1 * cos - t2 * sin, t2 * cos + t1 * sin, t[QK_HEAD_DIM:]], axis=0)

    cq_t = _rms(c_q, g_q_ref[...]).T.astype(BF16)
    ckv_n = _rms(c_kv, g_kv_ref[...])
    q_t = _dot(w_uqt_ref[...], cq_t)
    vt = _dot(w_uvt_ref[...], ckv_n.T.astype(BF16))
    k = _dot(ckv_n.astype(BF16), w_uk_ref[...])
    kr = rope_t(kr.T).T
    for hd in range(N_HEADS):
        sl = slice(hd * LANES, (hd + 1) * LANES)
        qt = (rope_t(q_t[sl, :]) * Q_SCALE).astype(BF16)
        kb = (k[:, sl] + kr).astype(BF16)
        q_ref[hd] = qt
        k_ref[hd] = kb
        qf = qt.astype(F32)
        kf = kb.astype(F32)
        qn_ref[hd:hd + 1, :] = jnp.sum(qf * qf, axis=0, keepdims=True)
        k_sq = jnp.max(jnp.sum(kf * kf, axis=1, keepdims=True), axis=0, keepdims=True)
        kn_ref[0, hd:hd + 1, :] = jnp.broadcast_to(k_sq, (1, LANES))
    pad_row = lax.broadcasted_iota(jnp.int32, (VT_ROWS - V_HEAD_DIM, vt.shape[1]), 0)
    ones_pad = jnp.where(pad_row == 0, 1.0, 0.0).astype(BF16)
    for hd in range(N_HEADS):
        lo = hd * VT_ROWS
        v_ref[lo:lo + V_HEAD_DIM, :] = vt[hd * V_HEAD_DIM:(hd + 1) * V_HEAD_DIM, :].astype(BF16)
        v_ref[lo + V_HEAD_DIM:lo + VT_ROWS, :] = ones_pad


def _pre_call(x, pos, g_pre, w_in_p, g_q, w_uqt, g_kv, w_uk_p, w_uvt, inv_rows):
    s = x.shape[0]
    tm = PRE_TM
    row = lambda w: pl.BlockSpec((tm, w), lambda i: (i, 0))
    full = lambda a: pl.BlockSpec(a.shape, lambda i: (0,) * a.ndim)
    heads = lambda n: pl.BlockSpec((n, tm, LANES), lambda i: (0, i, 0))
    return pl.pallas_call(
        _pre_kernel,
        grid=(s // tm,),
        in_specs=[row(D_MODEL), pl.BlockSpec((1, tm), lambda i: (0, i)), full(g_pre),
                  full(w_in_p), full(g_q), full(w_uqt), full(g_kv), full(w_uk_p), full(w_uvt),
                  full(inv_rows)],
        out_specs=[pl.BlockSpec((N_HEADS, LANES, tm), lambda i: (0, 0, i)), heads(N_HEADS),
                   pl.BlockSpec((N_HEADS * VT_ROWS, tm), lambda i: (0, i)),
                   row(POOL_WIDTH), row(D_MODEL), row(D_MODEL),
                   pl.BlockSpec((N_HEADS, tm), lambda i: (0, i)),
                   pl.BlockSpec((1, N_HEADS, LANES), lambda i: (i, 0, 0))],
        out_shape=[jax.ShapeDtypeStruct((N_HEADS, LANES, s), BF16),
                   jax.ShapeDtypeStruct((N_HEADS, s, LANES), BF16),
                   jax.ShapeDtypeStruct((N_HEADS * VT_ROWS, s), BF16),
                   jax.ShapeDtypeStruct((s, POOL_WIDTH), F32),
                   jax.ShapeDtypeStruct((s, D_MODEL), F32),
                   jax.ShapeDtypeStruct((s, D_MODEL), F32),
                   jax.ShapeDtypeStruct((N_HEADS, s), F32),
                   jax.ShapeDtypeStruct((s // tm, N_HEADS, LANES), F32)],
        compiler_params=pltpu.CompilerParams(
            dimension_semantics=("parallel",), vmem_limit_bytes=VMEM_LIMIT),
        name="pre_proj",
    )(x, pos, g_pre, w_in_p, g_q, w_uqt, g_kv, w_uk_p, w_uvt, inv_rows)


def _attn_kernel(qt_ref, k_ref, vt_ref, qn_ref, kn_ref, kn_next_ref, o_ref, m_ref, acc_ref, flag_ref):
    j = pl.program_id(1)

    def head_rows(hd):
        return slice(hd * VT_ROWS, (hd + 1) * VT_ROWS)

    def within_headroom(kn_blk_ref):
        k_norm = jnp.sqrt(jnp.max(kn_blk_ref[...], axis=0)[:, :1])
        bound = jnp.sqrt(qn_ref[...]) * k_norm
        excess = jnp.max(bound - m_ref[...].reshape(bound.shape))
        return (excess <= EXP_HEADROOM).astype(jnp.int32)

    @pl.when(j == 0)
    def _init():
        for hd in range(N_HEADS):
            m_ref[hd] = _dot(k_ref[hd, 0:BF16_ROWS, :], qt_ref[hd])[0:1, :]
        acc_ref[...] = jnp.zeros(acc_ref.shape, F32)
        flag_ref[0] = within_headroom(kn_ref)

    streaming = flag_ref[0] == 1

    @pl.when(streaming)
    def _stream():
        flag_ref[0] = within_headroom(kn_next_ref)
        tk, tq = k_ref.shape[1], qt_ref.shape[2]
        units = [(hd, qc, kc) for qc in range(0, tq, MXU_DIM) for kc in range(0, tk, MXU_DIM)
                 for hd in range(N_HEADS)]
        scores, pvs = {}, {}
        for t in range(len(units) + UNIT_LEAD):
            if t < len(units):
                hd, qc, kc = units[t]
                scores[t] = _dot(k_ref[hd, kc:kc + MXU_DIM, :], qt_ref[hd, :, qc:qc + MXU_DIM])
            u = t - UNIT_LEAD
            if u >= 0:
                hd, qc, kc = units[u]
                rows = head_rows(hd)
                cols = slice(qc, qc + MXU_DIM)
                p = jnp.exp2(scores.pop(u) - m_ref[hd, :, cols]).astype(BF16)
                d = _dot(vt_ref[rows, kc:kc + MXU_DIM], p)
                pvs[hd, qc] = d if (hd, qc) not in pvs else pvs[hd, qc] + d
                if kc + MXU_DIM == tk:
                    acc_ref[rows, cols] = acc_ref[rows, cols] + pvs.pop((hd, qc))

    @pl.when(jnp.logical_not(streaming))
    def _two_pass():
        scores, probs, alphas = {}, {}, {}
        for t in range(N_HEADS + QK_LEAD + PV_LAG):
            if t < N_HEADS:
                scores[t] = _dot(k_ref[t], qt_ref[t])
            hd = t - QK_LEAD
            if 0 <= hd < N_HEADS:
                s = scores.pop(hd)
                m_prev = m_ref[hd]
                m_new = jnp.maximum(m_prev, jnp.max(s, axis=0, keepdims=True))
                alphas[hd] = jnp.exp2(m_prev - m_new)
                probs[hd] = jnp.exp2(s - m_new).astype(BF16)
                m_ref[hd] = m_new
            hd = t - QK_LEAD - PV_LAG
            if 0 <= hd < N_HEADS:
                rows = head_rows(hd)
                pv = _dot(vt_ref[rows, :], probs.pop(hd))
                acc_ref[rows, :] = alphas.pop(hd) * acc_ref[rows, :] + pv
        flag_ref[0] = within_headroom(kn_next_ref)

    @pl.when(j == pl.num_programs(1) - 1)
    def _finish():
        outs = []
        for hd in range(N_HEADS):
            lo = hd * VT_ROWS
            denom = acc_ref[lo + V_HEAD_DIM:lo + V_HEAD_DIM + 1, :]
            outs.append(acc_ref[lo:lo + V_HEAD_DIM, :] / denom)
        o_ref[...] = jnp.concatenate(outs, axis=0).T.astype(o_ref.dtype)


def _attn_call(qt, k, vt, qn, kn):
    s = k.shape[1]
    tq, tk = ATTN_TQ, ATTN_TK
    width = N_HEADS * V_HEAD_DIM
    kn_rows = tk // PRE_TM
    last_kv = s // tk - 1
    return pl.pallas_call(
        _attn_kernel,
        grid=(s // tq, s // tk),
        in_specs=[pl.BlockSpec((N_HEADS, LANES, tq), lambda i, j: (0, 0, i)),
                  pl.BlockSpec((N_HEADS, tk, LANES), lambda i, j: (0, j, 0)),
                  pl.BlockSpec((N_HEADS * VT_ROWS, tk), lambda i, j: (0, j)),
                  pl.BlockSpec((N_HEADS, tq), lambda i, j: (0, i)),
                  pl.BlockSpec((kn_rows, N_HEADS, LANES), lambda i, j: (j, 0, 0)),
                  pl.BlockSpec((kn_rows, N_HEADS, LANES),
                               lambda i, j: (jnp.minimum(j + 1, last_kv), 0, 0))],
        out_specs=pl.BlockSpec((tq, width), lambda i, j: (i, 0)),
        out_shape=jax.ShapeDtypeStruct((s, width), BF16),
        scratch_shapes=[pltpu.VMEM((N_HEADS, 1, tq), F32),
                        pltpu.VMEM((N_HEADS * VT_ROWS, tq), F32),
                        pltpu.SMEM((1,), jnp.int32)],
        compiler_params=pltpu.CompilerParams(
            dimension_semantics=("parallel", "arbitrary"), vmem_limit_bytes=VMEM_LIMIT),
        name="mla_attention",
    )(qt, k, vt, qn, kn, kn)


def _mix_ffn_kernel(x_ref, attn_ref, u_ref, up_ref, un_ref, ga_ref, gb_ref,
                    w_oa_ref, w_pg_ref, pscale_ref, w_ob_ref, w_out_ref, g_post_ref,
                    g_fpre_ref, w_gu_ref, w_dn_ref, g_fpost_ref,
                    o_ref, ext_ref, *, seq_len):
    i = pl.program_id(0)
    tm = u_ref.shape[0]
    ext_ref[0:HALO, :] = jnp.where(i > 0, up_ref[...], 0.0)
    ext_ref[HALO:HALO + tm, :] = u_ref[...]
    ext_ref[HALO + tm:, :] = jnp.where(i < pl.num_programs(0) - 1, un_ref[...], 0.0)

    t = i * tm + lax.broadcasted_iota(jnp.int32, (tm, 1), 0)
    n_ext = tm + 2 * HALO
    mixed = []
    for g, w in enumerate(POOL_WINDOWS):
        left = w // 2
        right = w - left - 1
        cols = slice(g * POOL_GROUP_DIM, (g + 1) * POOL_GROUP_DIM)
        acc = ext_ref[:, cols]
        span = 1
        while span < w:
            acc = acc + pltpu.roll(acc, n_ext - span, axis=0)
            span *= 2
        win = pltpu.roll(acc, n_ext - (HALO - left), axis=0)[:tm]
        cnt = (jnp.minimum(t + right, seq_len - 1) - jnp.maximum(t - left, 0) + 1).astype(F32)
        pooled = win / cnt - u_ref[:, cols]
        mixed.append(_dot(pooled.astype(BF16), w_pg_ref[g]))
    pool = jnp.concatenate(mixed, axis=1) * pscale_ref[...]

    a = _dot(attn_ref[...], w_oa_ref[...])
    b = _dot(pool.astype(BF16), w_ob_ref[...])
    merged = ga_ref[...] * a + gb_ref[...] * b
    y = _dot(merged.astype(BF16), w_out_ref[...])
    x1 = x_ref[...] + _rms(y, g_post_ref[...])

    hf = _rms(x1, g_fpre_ref[...]).astype(BF16)
    gu = _dot(hf, w_gu_ref[...])
    act = jax.nn.silu(gu[:, :D_FF]) * gu[:, D_FF:]
    ff = _dot(act.astype(BF16), w_dn_ref[...])
    o_ref[...] = x1 + _rms(ff, g_fpost_ref[...])


def _mix_ffn_call(x, attn, u, ga, gb, w_oa, w_pg, pscale, w_ob, w_out, g_post,
                  g_fpre, w_gu, w_dn, g_fpost):
    s = x.shape[0]
    tm = MIX_TM
    nb = tm // HALO
    last = s // HALO - 1
    row = lambda w: pl.BlockSpec((tm, w), lambda i: (i, 0))
    full = lambda a: pl.BlockSpec(a.shape, lambda i: (0,) * a.ndim, pipeline_mode=pl.Buffered(1))
    prev = pl.BlockSpec((HALO, POOL_WIDTH), lambda i: (jnp.maximum(i * nb - 1, 0), 0))
    nxt = pl.BlockSpec((HALO, POOL_WIDTH), lambda i: (jnp.minimum((i + 1) * nb, last), 0))
    consts = (w_oa, w_pg, pscale, w_ob, w_out, g_post, g_fpre, w_gu, w_dn, g_fpost)
    return pl.pallas_call(
        functools.partial(_mix_ffn_kernel, seq_len=s),
        grid=(s // tm,),
        in_specs=[row(D_MODEL), row(POOL_WIDTH), row(POOL_WIDTH), prev, nxt,
                  row(D_MODEL), row(D_MODEL)] + [full(c) for c in consts],
        out_specs=row(D_MODEL),
        out_shape=jax.ShapeDtypeStruct((s, D_MODEL), F32),
        scratch_shapes=[pltpu.VMEM((tm + 2 * HALO, POOL_WIDTH), F32)],
        compiler_params=pltpu.CompilerParams(
            dimension_semantics=("parallel",), vmem_limit_bytes=VMEM_LIMIT),
        name="mix_ffn",
    )(x, attn, u, u, u, ga, gb, *consts)


def _head_tiles(w, per_head, keep):
    r = w.shape[0]
    w = w.reshape(r, N_HEADS, per_head)[:, :, :keep]
    return jnp.pad(w, ((0, 0), (0, 0), (0, LANES - keep))).reshape(r, HEAD_W)


def kernel(x, positions, g_mix_pre, w_in, g_q_lat, w_uq, g_kv_lat, w_ukv, w_o_attn,
           w_pool_group, pool_scale, w_o_pool, w_out, g_mix_post, g_ffn_pre,
           w_gate_up, w_down, g_ffn_post):
    b, s, d = x.shape
    assert b == 1 and d == D_MODEL
    assert s % max(PRE_TM, ATTN_TQ, ATTN_TK, MIX_TM) == 0
    vec = lambda g: g.reshape(1, -1).astype(F32)

    o_kr = Q_LORA_RANK + KV_LORA_RANK
    zeros = lambda n: jnp.zeros((D_MODEL, n), BF16)
    w_in_b = w_in.astype(BF16)
    w_in_p = jnp.concatenate(
        [w_in_b[:, :o_kr], zeros(QK_NOPE_DIM), w_in_b[:, o_kr:o_kr + QK_ROPE_DIM],
         zeros(LANES - QK_HEAD_DIM), w_in_b[:, o_kr + QK_ROPE_DIM:]], axis=1)
    w_uqt = _head_tiles(w_uq, QK_HEAD_DIM, QK_HEAD_DIM).T.astype(BF16)
    w_uk_p = _head_tiles(w_ukv, QK_NOPE_DIM + V_HEAD_DIM, QK_NOPE_DIM).astype(BF16)
    w_uv = w_ukv.reshape(KV_LORA_RANK, N_HEADS, QK_NOPE_DIM + V_HEAD_DIM)[:, :, QK_NOPE_DIM:]
    w_uvt = w_uv.reshape(KV_LORA_RANK, N_HEADS * V_HEAD_DIM).T.astype(BF16)

    inv = 1.0 / (ROPE_BASE ** (jnp.arange(0, QK_ROPE_DIM, 2, dtype=F32) / QK_ROPE_DIM))
    inv_rows = inv.reshape(QK_ROPE_DIM // 2, 1)

    x2 = x.reshape(s, d)
    q, k, v, u, ga, gb, qn, kn = _pre_call(
        x2, positions.reshape(1, s), vec(g_mix_pre), w_in_p, vec(g_q_lat), w_uqt,
        vec(g_kv_lat), w_uk_p, w_uvt, inv_rows)
    attn = _attn_call(q, k, v, qn, kn)
    out = _mix_ffn_call(
        x2, attn, u, ga, gb, w_o_attn.astype(BF16), w_pool_group.astype(BF16), vec(pool_scale),
        w_o_pool.astype(BF16), w_out.astype(BF16), vec(g_mix_post),
        vec(g_ffn_pre), w_gate_up.astype(BF16), w_down.astype(BF16), vec(g_ffn_post))
    return out.reshape(b, s, d)
```
